```python
import math
import jax, jax.numpy as jnp
from jax import lax
import numpy as np

D_MODEL = 4096
BATCH = 4
SEQ = 2048
DEPTH = 1
DEC_BATCH = 32
DEC_SEQ = 16
PAST_LEN = 4096

CHUNK = 64
W_CONV = D_MODEL // 2
W_LRU = D_MODEL // 2
N_LRU_HEADS = 16
LRU_HEAD_DIM = W_LRU // N_LRU_HEADS
K_CONFORMER = 31
K_LRU = 4
K_FFN = 3
D_FF = ((8 * D_MODEL // 3 + 255) // 256) * 256
LRU_C = 8.0
EPS = 1e-6
D_IN = 2 * W_CONV + 2 * W_LRU + 2 * D_MODEL

kernel_name = 'hybrid_conformer_rglru_stream_step'


def rmsnorm(x, g):
    xf = x.astype(jnp.float32)
    y = xf * lax.rsqrt(jnp.mean(xf * xf, axis=-1, keepdims=True) + EPS)
    return (y * g.astype(jnp.float32)).astype(x.dtype)


def layernorm(x, g, b):
    xf = x.astype(jnp.float32)
    mu = jnp.mean(xf, axis=-1, keepdims=True)
    var = jnp.mean(jnp.square(xf - mu), axis=-1, keepdims=True)
    y = (xf - mu) * lax.rsqrt(var + EPS) * g.astype(jnp.float32) + b.astype(jnp.float32)
    return y.astype(x.dtype)


def causal_dwconv(x, buf, w, b):
    k = w.shape[0]
    xp = jnp.concatenate([buf.astype(x.dtype), x], axis=1)
    y = lax.conv_general_dilated(xp, w[:, None, :].astype(x.dtype), window_strides=(1,),
                                 padding='VALID', dimension_numbers=('NWC', 'WIO', 'NWC'),
                                 feature_group_count=x.shape[-1])
    new_buf = xp[:, xp.shape[1] - (k - 1):]
    return y + b.astype(x.dtype), new_buf.astype(buf.dtype)


def _lin_combine(left, right):
    a1, b1 = left
    a2, b2 = right
    return a1 * a2, a2 * b1 + b2


def rg_lru(x, h0, w_r, b_r, w_i, b_i, lam):
    B, T, W = x.shape
    xh = x.reshape(B, T, N_LRU_HEADS, LRU_HEAD_DIM)
    r = jax.nn.sigmoid(jnp.einsum('bthd,hde->bthe', xh, w_r).reshape(B, T, W).astype(jnp.float32)
                       + b_r.astype(jnp.float32))
    i = jax.nn.sigmoid(jnp.einsum('bthd,hde->bthe', xh, w_i).reshape(B, T, W).astype(jnp.float32)
                       + b_i.astype(jnp.float32))
    log_a = -LRU_C * r * jax.nn.softplus(-lam.astype(jnp.float32))
    a = jnp.exp(log_a)
    u = jnp.sqrt(-jnp.expm1(2.0 * log_a)) * (i * x.astype(jnp.float32))
    blk = math.gcd(T, CHUNK)
    n = T // blk
    a_c = a.reshape(B, n, blk, W)
    u_c = u.reshape(B, n, blk, W)
    a_cum, u_cum = lax.associative_scan(_lin_combine, (a_c, u_c), axis=2)

    def step(h, inp):
        ac, uc = inp
        hs = ac * h[:, None, :] + uc
        return hs[:, -1], hs

    h_last, hs = lax.scan(step, h0.astype(jnp.float32),
                          (jnp.moveaxis(a_cum, 1, 0), jnp.moveaxis(u_cum, 1, 0)))
    y = jnp.moveaxis(hs, 0, 1).reshape(B, T, W)
    return y.astype(x.dtype), h_last.astype(h0.dtype)


def layer(x, s_conv_a, s_conv_b, s_lru, s_ffn,
          g_pre_mix, w_in, w_dw_a, b_dw_a, ln_a_g, ln_a_b, w_a_out,
          w_dw_b, b_dw_b, w_rg_r, b_rg_r, w_rg_i, b_rg_i, lru_lambda, w_b_out,
          w_o, g_post_mix, g_pre_ffn, w_up, w_dw_f, b_dw_f, w_down, g_post_ffn):
    h = rmsnorm(x, g_pre_mix)
    z = h @ w_in
    o1 = W_CONV
    o2 = o1 + W_CONV
    o3 = o2 + W_LRU
    o4 = o3 + W_LRU
    o5 = o4 + D_MODEL
    a_val, a_gate = z[..., :o1], z[..., o1:o2]
    b_x, b_gate = z[..., o2:o3], z[..., o3:o4]
    m_a, m_b = z[..., o4:o5], z[..., o5:]
    ua = a_val * jax.nn.sigmoid(a_gate)
    ua, new_sa = causal_dwconv(ua, s_conv_a, w_dw_a, b_dw_a)
    ua = jax.nn.silu(layernorm(ua, ln_a_g, ln_a_b))
    out_a = ua @ w_a_out
    xb, new_sb = causal_dwconv(b_x, s_conv_b, w_dw_b, b_dw_b)
    hb, new_h = rg_lru(xb, s_lru, w_rg_r, b_rg_r, w_rg_i, b_rg_i, lru_lambda)
    out_b = (jax.nn.gelu(b_gate) * hb) @ w_b_out
    mixed = jax.nn.sigmoid(m_a) * out_a + jax.nn.sigmoid(m_b) * out_b
    x = x + rmsnorm(mixed @ w_o, g_post_mix)
    h2 = rmsnorm(x, g_pre_ffn)
    up = h2 @ w_up
    up, new_sf = causal_dwconv(up, s_ffn, w_dw_f, b_dw_f)
    f = (jax.nn.gelu(up[..., :D_FF]) * up[..., D_FF:]) @ w_down
    x = x + rmsnorm(f, g_post_ffn)
    return x, new_sa, new_sb, new_h, new_sf


def setup_inputs(seed: int = 0) -> dict:
    key = jax.random.key(seed)
    ks = jax.random.split(key, 32)

    def nrm(k, shape, scale):
        return jax.random.normal(k, shape, jnp.float32) * scale

    L = DEPTH
    a0 = jax.random.uniform(ks[20], (L, W_LRU), jnp.float32, 0.9, 0.999)
    a_base = a0 ** (1.0 / LRU_C)
    lru_lambda = jnp.log(a_base) - jnp.log1p(-a_base)
    return {
        'x_prompt': nrm(ks[0], (BATCH, SEQ, D_MODEL), 1.0),
        'x_sample': nrm(ks[1], (DEC_BATCH, DEC_SEQ, D_MODEL), 1.0),
        'state_conv_a': nrm(ks[2], (L, DEC_BATCH, K_CONFORMER - 1, W_CONV), 1.0),
        'state_conv_b': nrm(ks[3], (L, DEC_BATCH, K_LRU - 1, W_LRU), 1.0),
        'state_lru': nrm(ks[4], (L, DEC_BATCH, W_LRU), 0.5),
        'state_ffn': nrm(ks[5], (L, DEC_BATCH, K_FFN - 1, 2 * D_FF), 1.0),
        'g_pre_mix': 1.0 + nrm(ks[6], (L, D_MODEL), 0.02),
        'w_in': nrm(ks[7], (L, D_MODEL, D_IN), D_MODEL ** -0.5),
        'w_dw_a': nrm(ks[8], (L, K_CONFORMER, W_CONV), K_CONFORMER ** -0.5),
        'b_dw_a': nrm(ks[9], (L, W_CONV), 0.01),
        'ln_a_g': 1.0 + nrm(ks[10], (L, W_CONV), 0.02),
        'ln_a_b': nrm(ks[11], (L, W_CONV), 0.01),
        'w_a_out': nrm(ks[12], (L, W_CONV, D_MODEL), W_CONV ** -0.5),
        'w_dw_b': nrm(ks[13], (L, K_LRU, W_LRU), K_LRU ** -0.5),
        'b_dw_b': nrm(ks[14], (L, W_LRU), 0.01),
        'w_rg_r': nrm(ks[15], (L, N_LRU_HEADS, LRU_HEAD_DIM, LRU_HEAD_DIM), LRU_HEAD_DIM ** -0.5),
        'b_rg_r': nrm(ks[16], (L, W_LRU), 0.01),
        'w_rg_i': nrm(ks[17], (L, N_LRU_HEADS, LRU_HEAD_DIM, LRU_HEAD_DIM), LRU_HEAD_DIM ** -0.5),
        'b_rg_i': nrm(ks[18], (L, W_LRU), 0.01),
        'lru_lambda': lru_lambda,
        'w_b_out': nrm(ks[19], (L, W_LRU, D_MODEL), W_LRU ** -0.5),
        'w_o': nrm(ks[21], (L, D_MODEL, D_MODEL), D_MODEL ** -0.5),
        'g_post_mix': 1.0 + nrm(ks[22], (L, D_MODEL), 0.02),
        'g_pre_ffn': 1.0 + nrm(ks[23], (L, D_MODEL), 0.02),
        'w_up': nrm(ks[24], (L, D_MODEL, 2 * D_FF), D_MODEL ** -0.5),
        'w_dw_f': nrm(ks[25], (L, K_FFN, 2 * D_FF), K_FFN ** -0.5),
        'b_dw_f': nrm(ks[26], (L, 2 * D_FF), 0.01),
        'w_down': nrm(ks[27], (L, D_FF, D_MODEL), D_FF ** -0.5),
        'g_post_ffn': 1.0 + nrm(ks[28], (L, D_MODEL), 0.02),
    }


def reference(x_prompt, x_sample, state_conv_a, state_conv_b, state_lru, state_ffn,
              g_pre_mix, w_in, w_dw_a, b_dw_a, ln_a_g, ln_a_b, w_a_out,
              w_dw_b, b_dw_b, w_rg_r, b_rg_r, w_rg_i, b_rg_i, lru_lambda, w_b_out,
              w_o, g_post_mix, g_pre_ffn, w_up, w_dw_f, b_dw_f, w_down, g_post_ffn):
    dt = x_prompt.dtype
    yp = x_prompt
    ys = x_sample
    pa, pb, ph, pf = [], [], [], []
    sa, sb, sh, sf = [], [], [], []
    for l in range(DEPTH):
        w_l = (g_pre_mix[l], w_in[l], w_dw_a[l], b_dw_a[l], ln_a_g[l], ln_a_b[l], w_a_out[l],
               w_dw_b[l], b_dw_b[l], w_rg_r[l], b_rg_r[l], w_rg_i[l], b_rg_i[l], lru_lambda[l],
               w_b_out[l], w_o[l], g_post_mix[l], g_pre_ffn[l], w_up[l], w_dw_f[l], b_dw_f[l],
               w_down[l], g_post_ffn[l])
        z_a = jnp.zeros((BATCH, K_CONFORMER - 1, W_CONV), dt)
        z_b = jnp.zeros((BATCH, K_LRU - 1, W_LRU), dt)
        z_h = jnp.zeros((BATCH, W_LRU), dt)
        z_f = jnp.zeros((BATCH, K_FFN - 1, 2 * D_FF), dt)
        yp, n_a, n_b, n_h, n_f = layer(yp, z_a, z_b, z_h, z_f, *w_l)
        pa.append(n_a); pb.append(n_b); ph.append(n_h); pf.append(n_f)
        ys, m_a, m_b, m_h, m_f = layer(ys, state_conv_a[l], state_conv_b[l], state_lru[l],
                                       state_ffn[l], *w_l)
        sa.append(m_a); sb.append(m_b); sh.append(m_h); sf.append(m_f)
    return (yp, ys,
            jnp.stack(pa), jnp.stack(pb), jnp.stack(ph), jnp.stack(pf),
            jnp.stack(sa), jnp.stack(sb), jnp.stack(sh), jnp.stack(sf))
```

```python
import functools
import math

import jax
import jax.numpy as jnp
from jax import lax
from jax.experimental import pallas as pl
from jax.experimental.pallas import tpu as pltpu

F32 = jnp.float32
BF16 = jnp.bfloat16

D_MODEL = 4096
W_BR = D_MODEL // 2
N_HEADS = 16
HEAD = W_BR // N_HEADS
K_A = 31
K_B = 4
K_F = 3
D_FF = 11008
D_FF_PAD = 11264
LRU_C = 8.0
EPS = 1e-6

V7X_VMEM_LIMIT = 56 * 1024 * 1024
SUBLANES = 8
CT = 256
CONV_ROWS = 32
OFF_A = 32
OFF_S = 8


def _params(sem):
    return pltpu.CompilerParams(dimension_semantics=sem, vmem_limit_bytes=V7X_VMEM_LIMIT)


def _dot(a, b):
    return jnp.dot(a, b, preferred_element_type=F32)


def _sigmoid(x):
    return 1.0 / (1.0 + jnp.exp(-x))


def _gelu(x):
    c = math.sqrt(2.0 / math.pi)
    return x * (0.5 * (1.0 + jnp.tanh(c * (x + 0.044715 * (x * x * x)))))


def _once(shape, index_map):
    return pl.BlockSpec(shape, index_map, pipeline_mode=pl.Buffered(1))


def _rms_cast_body(x_ref, g_ref, o_ref):
    x = x_ref[...]
    y = x * lax.rsqrt(jnp.mean(x * x, axis=-1, keepdims=True) + EPS)
    o_ref[...] = (y * g_ref[...]).astype(o_ref.dtype)


def _rms_cast(x, g, tm):
    m, d = x.shape
    return pl.pallas_call(
        _rms_cast_body,
        grid=(m // tm,),
        in_specs=[pl.BlockSpec((tm, d), lambda i: (i, 0)),
                  pl.BlockSpec((1, d), lambda i: (0, 0))],
        out_specs=pl.BlockSpec((tm, d), lambda i: (i, 0)),
        out_shape=jax.ShapeDtypeStruct((m, d), BF16),
        compiler_params=_params(("parallel",)),
        name="rms_cast",
    )(x, g)


def _resid_rms_body(x_ref, p_ref, g_ref, gn_ref, o_ref, h_ref):
    p = p_ref[...]
    y = p * lax.rsqrt(jnp.mean(p * p, axis=-1, keepdims=True) + EPS)
    x1 = x_ref[...] + y * g_ref[...]
    o_ref[...] = x1
    if h_ref is not None:
        h = x1 * lax.rsqrt(jnp.mean(x1 * x1, axis=-1, keepdims=True) + EPS)
        h_ref[...] = (h * gn_ref[...]).astype(h_ref.dtype)


def _resid_rms(x, p, g, g_next, tm):
    m, d = x.shape
    row = pl.BlockSpec((tm, d), lambda i: (i, 0))
    vec = pl.BlockSpec((1, d), lambda i: (0, 0))
    if g_next is None:
        body = lambda x_ref, p_ref, g_ref, o_ref: _resid_rms_body(x_ref, p_ref, g_ref, None, o_ref, None)
        return pl.pallas_call(
            body, grid=(m // tm,), in_specs=[row, row, vec], out_specs=row,
            out_shape=jax.ShapeDtypeStruct((m, d), F32),
            compiler_params=_params(("parallel",)), name="resid_rms",
        )(x, p, g)
    return pl.pallas_call(
        _resid_rms_body, grid=(m // tm,), in_specs=[row, row, vec, vec], out_specs=[row, row],
        out_shape=[jax.ShapeDtypeStruct((m, d), F32), jax.ShapeDtypeStruct((m, d), BF16)],
        compiler_params=_params(("parallel",)), name="resid_rms_next",
    )(x, p, g, g_next)


def _mm_body(a_ref, b_ref, o_ref, acc_ref, *, nk):
    if nk == 1:
        o_ref[...] = _dot(a_ref[...], b_ref[...])
        return
    k = pl.program_id(2)

    @pl.when(k == 0)
    def _():
        acc_ref[...] = _dot(a_ref[...], b_ref[...])

    @pl.when(jnp.logical_and(k > 0, k < nk - 1))
    def _():
        acc_ref[...] += _dot(a_ref[...], b_ref[...])

    @pl.when(k == nk - 1)
    def _():
        o_ref[...] = acc_ref[...] + _dot(a_ref[...], b_ref[...])


def _matmul(a, b, tm, tn, tk, name):
    m, kd = a.shape
    _, n = b.shape
    nk = kd // tk
    return pl.pallas_call(
        functools.partial(_mm_body, nk=nk),
        grid=(m // tm, n // tn, nk),
        in_specs=[pl.BlockSpec((tm, tk), lambda i, j, k: (i, k)),
                  pl.BlockSpec((tk, tn), lambda i, j, k: (k, j))],
        out_specs=pl.BlockSpec((tm, tn), lambda i, j, k: (i, j)),
        out_shape=jax.ShapeDtypeStruct((m, n), F32),
        scratch_shapes=[pltpu.VMEM((tm, tn) if nk > 1 else (SUBLANES, 128), F32)],
        compiler_params=_params(("parallel", "parallel", "arbitrary")),
        name=name,
    )(a, b)


def _proj_sigmoid_body(h_ref, w_ref, o_ref):
    o_ref[...] = _sigmoid(_dot(h_ref[...], w_ref[...])).astype(o_ref.dtype)


def _proj_sigmoid(h, w, col0, ncols, tm, tn):
    m, d = h.shape
    jb = col0 // tn
    return pl.pallas_call(
        _proj_sigmoid_body,
        grid=(m // tm, ncols // tn),
        in_specs=[_once((tm, d), lambda i, j: (i, 0)),
                  pl.BlockSpec((d, tn), lambda i, j: (0, jb + j))],
        out_specs=pl.BlockSpec((tm, tn), lambda i, j: (i, j)),
        out_shape=jax.ShapeDtypeStruct((m, ncols), BF16),
        compiler_params=_params(("parallel", "arbitrary")),
        name="proj_gates",
    )(h, w)


def _fill_conv_scratch(xp_ref, x, state_ref, k_taps, off, nb, t):
    xp_ref[:, off - (k_taps - 1):off, :] = state_ref[...]
    xp_ref[:, off:off + t, :] = x.reshape(nb, t, x.shape[-1])


def _conv_small(xp_ref, w_ref, b_ref, k_taps, off, t):
    base = off - (k_taps - 1)
    acc = b_ref[...][None] + w_ref[0:1, :][None] * xp_ref[:, base:base + t, :]
    for k in range(1, k_taps):
        acc = acc + w_ref[k:k + 1, :][None] * xp_ref[:, base + k:base + k + t, :]
    return acc


def _conv_chunked(xp_ref, o_ref, w_ref, b_ref, k_taps, off, nb, t):
    base = off - (k_taps - 1)
    rows = min(CONV_ROWS, t)
    per_seq = t // rows
    bias = b_ref[...]

    def chunk(c, carry):
        b = c // per_seq
        r0 = pl.multiple_of((c % per_seq) * rows, rows)
        win = xp_ref[b, pl.ds(r0, rows + off), :]
        acc = bias + w_ref[0:1, :] * win[base:base + rows]
        for k in range(1, k_taps):
            acc = acc + w_ref[k:k + 1, :] * win[base + k:base + k + rows]
        o_ref[pl.ds(pl.multiple_of(b * t + r0, rows), rows), :] = acc
        return carry

    lax.fori_loop(0, nb * per_seq, chunk, 0)


def _proj_a_body(h_ref, w_ref, st_ref, wdw_ref, bdw_ref, uc_ref, nst_ref, xp_ref, *, nb, t):
    r = _dot(h_ref[...], w_ref[...])
    ua = r[:, :CT] * _sigmoid(r[:, CT:])
    _fill_conv_scratch(xp_ref, ua, st_ref, K_A, OFF_A, nb, t)
    _conv_chunked(xp_ref, uc_ref, wdw_ref, bdw_ref, K_A, OFF_A, nb, t)
    nst_ref[...] = xp_ref[:, OFF_A + t - (K_A - 1):OFF_A + t, :]


def _proj_a(h, w, col0, state, wdw, bdw, nb, t):
    m, d = h.shape
    tm = nb * t
    nseq = state.shape[0]
    jb = col0 // (2 * CT)
    return pl.pallas_call(
        functools.partial(_proj_a_body, nb=nb, t=t),
        grid=(m // tm, W_BR // CT),
        in_specs=[_once((tm, d), lambda i, j: (i, 0)),
                  pl.BlockSpec((d, 2 * CT), lambda i, j: (0, jb + j)),
                  pl.BlockSpec((nb, K_A - 1, CT), lambda i, j: (i, 0, j)),
                  pl.BlockSpec((K_A, CT), lambda i, j: (0, j)),
                  pl.BlockSpec((1, CT), lambda i, j: (0, j))],
        out_specs=[pl.BlockSpec((tm, CT), lambda i, j: (i, j)),
                   pl.BlockSpec((nb, K_A - 1, CT), lambda i, j: (i, 0, j))],
        out_shape=[jax.ShapeDtypeStruct((m, W_BR), F32),
                   jax.ShapeDtypeStruct((nseq, K_A - 1, W_BR), F32)],
        scratch_shapes=[pltpu.VMEM((nb, OFF_A + t, CT), F32)],
        compiler_params=_params(("parallel", "arbitrary")),
        name="proj_a",
    )(h, w, state, wdw, bdw)


def _shift_rows(x, d, fill, row):
    return jnp.where(row >= d, pltpu.roll(x, d, axis=1), fill)


def _proj_b_body(h_ref, w_ref, st_ref, h0_ref, wdw_ref, bdw_ref, wr_ref, br_ref, wi_ref, bi_ref,
                 lam_ref, yb_ref, nst_ref, nh_ref, xp_ref, a_ref, u_ref, *, nb, t):
    tm = nb * t
    r = _dot(h_ref[...], w_ref[...])
    bx = r[:, :CT]
    bg = r[:, CT:]
    _fill_conv_scratch(xp_ref, bx, st_ref, K_B, OFF_S, nb, t)
    nst_ref[...] = xp_ref[:, OFF_S + t - (K_B - 1):OFF_S + t, :]
    xb = _conv_small(xp_ref, wdw_ref, bdw_ref, K_B, OFF_S, t).reshape(tm, CT)

    xb16 = xb.astype(BF16)
    r_lin = jnp.concatenate(
        [_dot(xb16[:, hd * HEAD:(hd + 1) * HEAD], wr_ref[hd]) for hd in range(CT // HEAD)], axis=1)
    i_lin = jnp.concatenate(
        [_dot(xb16[:, hd * HEAD:(hd + 1) * HEAD], wi_ref[hd]) for hd in range(CT // HEAD)], axis=1)
    rg = _sigmoid(r_lin + br_ref[...])
    ig = _sigmoid(i_lin + bi_ref[...])
    neg_lam = -lam_ref[...]
    softplus = jnp.maximum(neg_lam, 0.0) + jnp.log1p(jnp.exp(-jnp.abs(neg_lam)))
    log_a = (-LRU_C) * rg * softplus
    a_ref[...] = jnp.exp(log_a).reshape(nb, t, CT)
    th = jnp.tanh(log_a)
    u_ref[...] = (jnp.sqrt(-2.0 * th / (1.0 - th)) * (ig * xb)).reshape(nb, t, CT)

    row = lax.broadcasted_iota(jnp.int32, (nb, SUBLANES, CT), 1)

    def group(c, hprev):
        r0 = pl.multiple_of(c * SUBLANES, SUBLANES)
        a = a_ref[:, pl.ds(r0, SUBLANES), :]
        u = u_ref[:, pl.ds(r0, SUBLANES), :]
        for d in (1, 2, 4):
            u = a * _shift_rows(u, d, 0.0, row) + u
            a = a * _shift_rows(a, d, 1.0, row)
        hs = a * hprev[:, None, :] + u
        u_ref[:, pl.ds(r0, SUBLANES), :] = hs
        return hs[:, SUBLANES - 1, :]

    h_last = lax.fori_loop(0, t // SUBLANES, group, h0_ref[:, 0, :])
    nh_ref[:, 0, :] = h_last
    yb_ref[...] = (_gelu(bg) * u_ref[...].reshape(tm, CT)).astype(yb_ref.dtype)


def _proj_b(h, w, col0, state, h0, wdw, bdw, wr, br, wi, bi, lam, nb, t):
    m, d = h.shape
    tm = nb * t
    nseq = state.shape[0]
    jb = col0 // (2 * CT)
    hpt = CT // HEAD
    vec = pl.BlockSpec((1, CT), lambda i, j: (0, j))
    return pl.pallas_call(
        functools.partial(_proj_b_body, nb=nb, t=t),
        grid=(m // tm, W_BR // CT),
        in_specs=[_once((tm, d), lambda i, j: (i, 0)),
                  pl.BlockSpec((d, 2 * CT), lambda i, j: (0, jb + j)),
                  pl.BlockSpec((nb, K_B - 1, CT), lambda i, j: (i, 0, j)),
                  pl.BlockSpec((nb, 1, CT), lambda i, j: (i, 0, j)),
                  pl.BlockSpec((K_B, CT), lambda i, j: (0, j)),
                  vec,
                  pl.BlockSpec((hpt, HEAD, HEAD), lambda i, j: (j, 0, 0)),
                  vec,
                  pl.BlockSpec((hpt, HEAD, HEAD), lambda i, j: (j, 0, 0)),
                  vec,
                  vec],
        out_specs=[pl.BlockSpec((tm, CT), lambda i, j: (i, j)),
                   pl.BlockSpec((nb, K_B - 1, CT), lambda i, j: (i, 0, j)),
                   pl.BlockSpec((nb, 1, CT), lambda i, j: (i, 0, j))],
        out_shape=[jax.ShapeDtypeStruct((m, W_BR), BF16),
                   jax.ShapeDtypeStruct((nseq, K_B - 1, W_BR), F32),
                   jax.ShapeDtypeStruct((nseq, 1, W_BR), F32)],
        scratch_shapes=[pltpu.VMEM((nb, OFF_S + t, CT), F32),
                        pltpu.VMEM((nb, t, CT), F32),
                        pltpu.VMEM((nb, t, CT), F32)],
        compiler_params=_params(("parallel", "arbitrary")),
        name="proj_b",
    )(h, w, state, h0, wdw, bdw, wr, br, wi, bi, lam)


def _merge_body(uc_ref, yb_ref, ga_ref, gb_ref, wa_ref, wb_ref, lng_ref, lnb_ref, o_ref, ya_ref):
    @pl.when(pl.program_id(1) == 0)
    def _():
        x = uc_ref[...]
        mu = jnp.mean(x, axis=-1, keepdims=True)
        xc = x - mu
        var = jnp.mean(xc * xc, axis=-1, keepdims=True)
        y = xc * lax.rsqrt(var + EPS) * lng_ref[...] + lnb_ref[...]
        ya_ref[...] = (y * _sigmoid(y)).astype(ya_ref.dtype)

    out_a = _dot(ya_ref[...], wa_ref[...])
    out_b = _dot(yb_ref[...], wb_ref[...])
    mixed = ga_ref[...].astype(F32) * out_a + gb_ref[...].astype(F32) * out_b
    o_ref[...] = mixed.astype(o_ref.dtype)


def _merge(uc, yb, gates, wa, wb, ln_g, ln_b, tm, tn):
    m = uc.shape[0]
    nj = D_MODEL // tn
    return pl.pallas_call(
        _merge_body,
        grid=(m // tm, nj),
        in_specs=[_once((tm, W_BR), lambda i, j: (i, 0)),
                  _once((tm, W_BR), lambda i, j: (i, 0)),
                  pl.BlockSpec((tm, tn), lambda i, j: (i, j)),
                  pl.BlockSpec((tm, tn), lambda i, j: (i, nj + j)),
                  pl.BlockSpec((W_BR, tn), lambda i, j: (0, j)),
                  pl.BlockSpec((W_BR, tn), lambda i, j: (0, j)),
                  pl.BlockSpec((1, W_BR), lambda i, j: (0, 0)),
                  pl.BlockSpec((1, W_BR), lambda i, j: (0, 0))],
        out_specs=pl.BlockSpec((tm, tn), lambda i, j: (i, j)),
        out_shape=jax.ShapeDtypeStruct((m, D_MODEL), BF16),
        scratch_shapes=[pltpu.VMEM((tm, W_BR), BF16)],
        compiler_params=_params(("parallel", "arbitrary")),
        name="merge",
    )(uc, yb, gates, gates, wa, wb, ln_g, ln_b)


def _ffn_up_body(h_ref, w_ref, st_ref, wdw_ref, bdw_ref, g_ref, nst_ref, xp_ref, *, nb, t):
    tm = nb * t
    r = _dot(h_ref[...], w_ref[...])
    _fill_conv_scratch(xp_ref, r, st_ref, K_F, OFF_S, nb, t)
    nst_ref[...] = xp_ref[:, OFF_S + t - (K_F - 1):OFF_S + t, :]
    c = _conv_small(xp_ref, wdw_ref, bdw_ref, K_F, OFF_S, t).reshape(tm, 2 * CT)
    g_ref[...] = (_gelu(c[:, :CT]) * c[:, CT:]).astype(g_ref.dtype)


def _ffn_up(h, w, state, wdw, bdw, nb, t):
    m, d = h.shape
    tm = nb * t
    nseq = state.shape[0]
    return pl.pallas_call(
        functools.partial(_ffn_up_body, nb=nb, t=t),
        grid=(m // tm, D_FF_PAD // CT),
        in_specs=[_once((tm, d), lambda i, j: (i, 0)),
                  pl.BlockSpec((d, 2 * CT), lambda i, j: (0, j)),
                  pl.BlockSpec((nb, K_F - 1, 2 * CT), lambda i, j: (i, 0, j)),
                  pl.BlockSpec((K_F, 2 * CT), lambda i, j: (0, j)),
                  pl.BlockSpec((1, 2 * CT), lambda i, j: (0, j))],
        out_specs=[pl.BlockSpec((tm, CT), lambda i, j: (i, j)),
                   pl.BlockSpec((nb, K_F - 1, 2 * CT), lambda i, j: (i, 0, j))],
        out_shape=[jax.ShapeDtypeStruct((m, D_FF_PAD), BF16),
                   jax.ShapeDtypeStruct((nseq, K_F - 1, 2 * D_FF_PAD), F32)],
        scratch_shapes=[pltpu.VMEM((nb, OFF_S + t, 2 * CT), F32)],
        compiler_params=_params(("parallel", "arbitrary")),
        name="ffn_up",
    )(h, w, state, wdw, bdw)


def _pair_cols(a, b):
    lead = a.shape[:-1]
    n = a.shape[-1] // CT
    ab = jnp.stack([a.reshape(*lead, n, CT), b.reshape(*lead, n, CT)], axis=-2)
    return ab.reshape(*lead, 2 * n * CT)


def _unpair_cols(x):
    lead = x.shape[:-1]
    n = x.shape[-1] // (2 * CT)
    x = x.reshape(*lead, n, 2, CT)
    return x[..., 0, :].reshape(*lead, n * CT), x[..., 1, :].reshape(*lead, n * CT)


def _pad_ff(a):
    pad = [(0, 0)] * (a.ndim - 1) + [(0, D_FF_PAD - D_FF)]
    return jnp.pad(a, pad)


def _pair_ff(a):
    return _pair_cols(_pad_ff(a[..., :D_FF]), _pad_ff(a[..., D_FF:]))


def _unpair_ff(x):
    a, b = _unpair_cols(x)
    return jnp.concatenate([a[..., :D_FF], b[..., :D_FF]], axis=-1)


def _layer(x3, s_a, s_b, s_h, s_f, wts, nb, tm_mm):
    bsz, t, d = x3.shape
    m = bsz * t
    x = x3.reshape(m, d)

    h = _rms_cast(x, wts["g_pre_mix"], 256)
    uc, new_sa = _proj_a(h, wts["w_in"], 0, s_a, wts["w_dw_a"], wts["b_dw_a"], nb, t)
    yb, new_sb, new_h = _proj_b(h, wts["w_in"], 2 * W_BR, s_b, s_h[:, None, :], wts["w_dw_b"], wts["b_dw_b"],
                                wts["w_rg_r"], wts["b_rg_r"], wts["w_rg_i"], wts["b_rg_i"],
                                wts["lru_lambda"], nb, t)
    gates = _proj_sigmoid(h, wts["w_in"], 4 * W_BR, 2 * D_MODEL, tm_mm, 1024)
    mixed = _merge(uc, yb, gates, wts["w_a_out"], wts["w_b_out"], wts["ln_a_g"], wts["ln_a_b"],
                   tm_mm, 512)
    p = _matmul(mixed, wts["w_o"], tm_mm, 1024, D_MODEL, "mm_o")
    x1, h2 = _resid_rms(x, p, wts["g_post_mix"], wts["g_pre_ffn"], 256)
    g, new_sf = _ffn_up(h2, wts["w_up"], s_f, wts["w_dw_f"], wts["b_dw_f"], nb, t)
    f = _matmul(g, wts["w_down"], tm_mm, 1024, D_FF_PAD // 4, "mm_down")
    y = _resid_rms(x1, f, wts["g_post_ffn"], None, 256)
    return y.reshape(bsz, t, d), new_sa, new_sb, new_h[:, 0, :], _unpair_ff(new_sf)


def kernel(x_prompt, x_sample, state_conv_a, state_conv_b, state_lru, state_ffn, g_pre_mix, w_in, w_dw_a, b_dw_a, ln_a_g, ln_a_b, w_a_out, w_dw_b, b_dw_b, w_rg_r, b_rg_r, w_rg_i, b_rg_i, lru_lambda, w_b_out, w_o, g_post_mix, g_pre_ffn, w_up, w_dw_f, b_dw_f, w_down, g_post_ffn):
    depth = w_in.shape[0]
    bp, tp, _ = x_prompt.shape
    bs, ts, _ = x_sample.shape
    yp, ys = x_prompt, x_sample
    outs_p = [[], [], [], []]
    outs_s = [[], [], [], []]
    for l in range(depth):
        wi = w_in[l]
        o1, o2, o3, o4 = W_BR, 2 * W_BR, 3 * W_BR, 4 * W_BR
        wts = {
            "g_pre_mix": g_pre_mix[l][None],
            "w_in": jnp.concatenate([_pair_cols(wi[:, :o1], wi[:, o1:o2]),
                                     _pair_cols(wi[:, o2:o3], wi[:, o3:o4]),
                                     wi[:, o4:]], axis=1).astype(BF16),
            "w_dw_a": w_dw_a[l], "b_dw_a": b_dw_a[l][None],
            "ln_a_g": ln_a_g[l][None], "ln_a_b": ln_a_b[l][None],
            "w_a_out": w_a_out[l].astype(BF16),
            "w_dw_b": w_dw_b[l], "b_dw_b": b_dw_b[l][None],
            "w_rg_r": w_rg_r[l].astype(BF16), "b_rg_r": b_rg_r[l][None],
            "w_rg_i": w_rg_i[l].astype(BF16), "b_rg_i": b_rg_i[l][None],
            "lru_lambda": lru_lambda[l][None],
            "w_b_out": w_b_out[l].astype(BF16),
            "w_o": w_o[l].astype(BF16),
            "g_post_mix": g_post_mix[l][None], "g_pre_ffn": g_pre_ffn[l][None],
            "w_up": _pair_ff(w_up[l]).astype(BF16),
            "w_dw_f": _pair_ff(w_dw_f[l]), "b_dw_f": _pair_ff(b_dw_f[l])[None],
            "w_down": jnp.pad(w_down[l], ((0, D_FF_PAD - D_FF), (0, 0))).astype(BF16),
            "g_post_ffn": g_post_ffn[l][None],
        }
        dt = x_prompt.dtype
        yp, *new_p = _layer(yp,
                            jnp.zeros((bp, K_A - 1, W_BR), dt), jnp.zeros((bp, K_B - 1, W_BR), dt),
                            jnp.zeros((bp, W_BR), dt), jnp.zeros((bp, K_F - 1, 2 * D_FF_PAD), dt),
                            wts, 1, 1024)
        ys, *new_s = _layer(ys, state_conv_a[l], state_conv_b[l], state_lru[l],
                            _pair_ff(state_ffn[l]), wts, bs, bs * ts)
        for acc, new in ((outs_p, new_p), (outs_s, new_s)):
            for lst, v in zip(acc, new):
                lst.append(v)
    return (yp, ys, *[jnp.stack(v) for v in outs_p], *[jnp.stack(v) for v in outs_s])
```

```python
import functools
import math

import jax
import jax.numpy as jnp
from jax import lax
from jax.experimental import pallas as pl
from jax.experimental.pallas import tpu as pltpu

F32 = jnp.float32
BF16 = jnp.bfloat16

D_MODEL = 4096
W_BR = D_MODEL // 2
N_HEADS = 16
HEAD = W_BR // N_HEADS
K_A = 31
K_B = 4
K_F = 3
D_FF = 11008
LRU_C = 8.0
EPS = 1e-6

V7X_VMEM_LIMIT = 58 * 1024 * 1024
SUBLANES = 8
CT = 256
NJ_FF = D_FF // CT
TK_DOWN = 1024
D_FF_PAD = pl.cdiv(D_FF, TK_DOWN) * TK_DOWN
CONV_ROWS = 32
OFF_A = 32
OFF_S = 8
TN = 512


def _params(sem):
    return pltpu.CompilerParams(dimension_semantics=sem, vmem_limit_bytes=V7X_VMEM_LIMIT)


def _dot(a, b):
    return jnp.dot(a, b, preferred_element_type=F32)


def _sigmoid(x):
    return 1.0 / (1.0 + jnp.exp(-x))


def _gelu(x):
    c = math.sqrt(2.0 / math.pi)
    return x * (0.5 * (1.0 + jnp.tanh(c * (x + 0.044715 * (x * x * x)))))


def _once(shape, index_map):
    return pl.BlockSpec(shape, index_map, pipeline_mode=pl.Buffered(1))


def _cast_pair(w_scr, wa_ref, wb_ref):
    n = wa_ref.shape[-1]
    w_scr[:, :n] = wa_ref[...].astype(BF16)
    w_scr[:, n:] = wb_ref[...].astype(BF16)


def _rms_cast_body(x_ref, g_ref, o_ref):
    x = x_ref[...]
    y = x * lax.rsqrt(jnp.mean(x * x, axis=-1, keepdims=True) + EPS)
    o_ref[...] = (y * g_ref[...]).astype(o_ref.dtype)


def _rms_cast(x, g, l, tm):
    m, d = x.shape
    return pl.pallas_call(
        _rms_cast_body,
        grid=(m // tm,),
        in_specs=[pl.BlockSpec((tm, d), lambda i: (i, 0)),
                  pl.BlockSpec((None, 1, d), lambda i: (l, 0, 0))],
        out_specs=pl.BlockSpec((tm, d), lambda i: (i, 0)),
        out_shape=jax.ShapeDtypeStruct((m, d), BF16),
        compiler_params=_params(("parallel",)),
        name="rms_cast",
    )(x, g)


def _resid_rms_body(x_ref, p_ref, g_ref, o_ref):
    p = p_ref[...]
    y = p * lax.rsqrt(jnp.mean(p * p, axis=-1, keepdims=True) + EPS)
    o_ref[...] = x_ref[...] + y * g_ref[...]


def _resid_rms_next_body(x_ref, p_ref, g_ref, gn_ref, o_ref, h_ref):
    _resid_rms_body(x_ref, p_ref, g_ref, o_ref)
    x1 = o_ref[...]
    h = x1 * lax.rsqrt(jnp.mean(x1 * x1, axis=-1, keepdims=True) + EPS)
    h_ref[...] = (h * gn_ref[...]).astype(h_ref.dtype)


def _resid_rms(x, p, g, g_next, l, tm):
    m, d = x.shape
    row = pl.BlockSpec((tm, d), lambda i: (i, 0))
    vec = pl.BlockSpec((None, 1, d), lambda i: (l, 0, 0))
    if g_next is None:
        return pl.pallas_call(
            _resid_rms_body, grid=(m // tm,), in_specs=[row, row, vec], out_specs=row,
            out_shape=jax.ShapeDtypeStruct((m, d), F32),
            compiler_params=_params(("parallel",)), name="resid_rms",
        )(x, p, g)
    return pl.pallas_call(
        _resid_rms_next_body, grid=(m // tm,), in_specs=[row, row, vec, vec], out_specs=[row, row],
        out_shape=[jax.ShapeDtypeStruct((m, d), F32), jax.ShapeDtypeStruct((m, d), BF16)],
        compiler_params=_params(("parallel",)), name="resid_rms_next",
    )(x, p, g, g_next)


def _proj_body(a_ref, w_ref, o_ref, w_scr, *, act):
    w_scr[...] = w_ref[...].astype(BF16)
    r = _dot(a_ref[...], w_scr[...])
    if act is not None:
        r = act(r)
    o_ref[...] = r.astype(o_ref.dtype)


def _proj(a, w, l, col0, ncols, tm, act, out_dtype, name):
    m, d = a.shape
    jb = col0 // TN
    return pl.pallas_call(
        functools.partial(_proj_body, act=act),
        grid=(m // tm, ncols // TN),
        in_specs=[_once((tm, d), lambda i, j: (i, 0)),
                  pl.BlockSpec((None, d, TN), lambda i, j: (l, 0, jb + j))],
        out_specs=pl.BlockSpec((tm, TN), lambda i, j: (i, j)),
        out_shape=jax.ShapeDtypeStruct((m, ncols), out_dtype),
        scratch_shapes=[pltpu.VMEM((d, TN), BF16)],
        compiler_params=_params(("parallel", "arbitrary")),
        name=name,
    )(a, w)


def _down_body(a_ref, w_ref, o_ref, w_scr, *, k_rows):
    k = pl.program_id(2)

    @pl.when(k == 0)
    def _():
        o_ref[...] = jnp.zeros(o_ref.shape, o_ref.dtype)

    row = lax.broadcasted_iota(jnp.int32, w_ref.shape, 0)
    w_scr[...] = jnp.where(row < k_rows - k * TK_DOWN, w_ref[...], 0.0).astype(BF16)
    o_ref[...] += _dot(a_ref[...], w_scr[...])


def _ffn_down(g, w, l, tm, tn):
    m, kp = g.shape
    kd, n = w.shape[1:]
    nk = kp // TK_DOWN
    return pl.pallas_call(
        functools.partial(_down_body, k_rows=kd),
        grid=(m // tm, n // tn, nk),
        in_specs=[pl.BlockSpec((tm, TK_DOWN), lambda i, j, k: (i, k)),
                  pl.BlockSpec((None, TK_DOWN, tn), lambda i, j, k: (l, k, j))],
        out_specs=pl.BlockSpec((tm, tn), lambda i, j, k: (i, j)),
        out_shape=jax.ShapeDtypeStruct((m, n), F32),
        scratch_shapes=[pltpu.VMEM((TK_DOWN, tn), BF16)],
        compiler_params=_params(("parallel", "parallel", "arbitrary")),
        name="ffn_down",
    )(g, w)


def _fill_conv_scratch(xp_ref, x, state_ref, k_taps, off, nb, t):
    xp_ref[:, off - (k_taps - 1):off, :] = state_ref[...]
    xp_ref[:, off:off + t, :] = x.reshape(nb, t, x.shape[-1])


def _conv_small(xp_ref, w_ref, b_ref, k_taps, off, t):
    base = off - (k_taps - 1)
    acc = b_ref[...][None] + w_ref[0:1, :][None] * xp_ref[:, base:base + t, :]
    for k in range(1, k_taps):
        acc = acc + w_ref[k:k + 1, :][None] * xp_ref[:, base + k:base + k + t, :]
    return acc


def _conv_chunked(xp_ref, o_ref, w_ref, b_ref, k_taps, off, nb, t):
    base = off - (k_taps - 1)
    rows = min(CONV_ROWS, t)
    per_seq = t // rows
    wrows = rows + off
    bias = b_ref[...]

    def chunk(c, carry):
        b = c // per_seq
        r0 = pl.multiple_of((c % per_seq) * rows, rows)
        win = xp_ref[b, pl.ds(r0, wrows), :]
        acc = jnp.broadcast_to(bias, (rows, bias.shape[-1]))
        for s in range(SUBLANES):
            taps = [k for k in range(k_taps) if (base + k) % SUBLANES == s]
            if not taps:
                continue
            ws = win if s == 0 else pltpu.roll(win, wrows - s, axis=0)
            for k in taps:
                q = (base + k) - s
                acc = acc + w_ref[k:k + 1, :] * ws[q:q + rows]
        o_ref[pl.ds(pl.multiple_of(b * t + r0, rows), rows), :] = acc
        return carry

    lax.fori_loop(0, nb * per_seq, chunk, 0)


def _proj_a_body(h_ref, wv_ref, wg_ref, st_ref, wdw_ref, bdw_ref, uc_ref, nst_ref, w_scr, xp_ref,
                 *, nb, t):
    _cast_pair(w_scr, wv_ref, wg_ref)
    r = _dot(h_ref[...], w_scr[...])
    ua = r[:, :CT] * _sigmoid(r[:, CT:])
    _fill_conv_scratch(xp_ref, ua, st_ref, K_A, OFF_A, nb, t)
    _conv_chunked(xp_ref, uc_ref, wdw_ref, bdw_ref, K_A, OFF_A, nb, t)
    nst_ref[...] = xp_ref[:, OFF_A + t - (K_A - 1):OFF_A + t, :]


def _proj_a(h, w_in, l, ls, state, wdw, bdw, nb, t):
    m, d = h.shape
    tm = nb * t
    nseq = state.shape[1]
    nj = W_BR // CT
    return pl.pallas_call(
        functools.partial(_proj_a_body, nb=nb, t=t),
        grid=(m // tm, nj),
        in_specs=[_once((tm, d), lambda i, j: (i, 0)),
                  pl.BlockSpec((None, d, CT), lambda i, j: (l, 0, j)),
                  pl.BlockSpec((None, d, CT), lambda i, j: (l, 0, nj + j)),
                  pl.BlockSpec((None, nb, K_A - 1, CT), lambda i, j: (ls, i, 0, j)),
                  pl.BlockSpec((None, K_A, CT), lambda i, j: (l, 0, j)),
                  pl.BlockSpec((None, 1, CT), lambda i, j: (l, 0, j))],
        out_specs=[pl.BlockSpec((tm, CT), lambda i, j: (i, j)),
                   pl.BlockSpec((nb, K_A - 1, CT), lambda i, j: (i, 0, j))],
        out_shape=[jax.ShapeDtypeStruct((m, W_BR), F32),
                   jax.ShapeDtypeStruct((nseq, K_A - 1, W_BR), F32)],
        scratch_shapes=[pltpu.VMEM((d, 2 * CT), BF16),
                        pltpu.VMEM((nb, OFF_A + t, CT), F32)],
        compiler_params=_params(("parallel", "arbitrary")),
        name="proj_a",
    )(h, w_in, w_in, state, wdw, bdw)


def _shift_rows(x, d, fill, row):
    return jnp.where(row >= d, pltpu.roll(x, d, axis=1), fill)


def _proj_b_body(h_ref, wx_ref, wg_ref, st_ref, h0_ref, wdw_ref, bdw_ref, wr_ref, br_ref, wi_ref,
                 bi_ref, lam_ref, yb_ref, nst_ref, nh_ref, w_scr, xp_ref, a_ref, u_ref, *, nb, t):
    tm = nb * t
    _cast_pair(w_scr, wx_ref, wg_ref)
    r = _dot(h_ref[...], w_scr[...])
    bx = r[:, :CT]
    bg = r[:, CT:]
    _fill_conv_scratch(xp_ref, bx, st_ref, K_B, OFF_S, nb, t)
    nst_ref[...] = xp_ref[:, OFF_S + t - (K_B - 1):OFF_S + t, :]
    xb = _conv_small(xp_ref, wdw_ref, bdw_ref, K_B, OFF_S, t).reshape(tm, CT)

    xb16 = xb.astype(BF16)
    heads = range(CT // HEAD)
    r_lin = jnp.concatenate(
        [_dot(xb16[:, hd * HEAD:(hd + 1) * HEAD], wr_ref[hd].astype(BF16)) for hd in heads], axis=1)
    i_lin = jnp.concatenate(
        [_dot(xb16[:, hd * HEAD:(hd + 1) * HEAD], wi_ref[hd].astype(BF16)) for hd in heads], axis=1)
    rg = _sigmoid(r_lin + br_ref[...])
    ig = _sigmoid(i_lin + bi_ref[...])
    neg_lam = -lam_ref[...]
    softplus = jnp.maximum(neg_lam, 0.0) + jnp.log1p(jnp.exp(-jnp.abs(neg_lam)))
    log_a = (-LRU_C) * rg * softplus
    a_ref[...] = jnp.exp(log_a).reshape(nb, t, CT)
    th = jnp.tanh(log_a)
    u_ref[...] = (jnp.sqrt(-2.0 * th / (1.0 - th)) * (ig * xb)).reshape(nb, t, CT)

    row = lax.broadcasted_iota(jnp.int32, (nb, SUBLANES, CT), 1)

    def group(c, hprev):
        r0 = pl.multiple_of(c * SUBLANES, SUBLANES)
        a = a_ref[:, pl.ds(r0, SUBLANES), :]
        u = u_ref[:, pl.ds(r0, SUBLANES), :]
        for d in (1, 2, 4):
            u = a * _shift_rows(u, d, 0.0, row) + u
            a = a * _shift_rows(a, d, 1.0, row)
        hs = a * hprev[:, None, :] + u
        u_ref[:, pl.ds(r0, SUBLANES), :] = hs
        return hs[:, SUBLANES - 1, :]

    h_last = lax.fori_loop(0, t // SUBLANES, group, h0_ref[:, 0, :])
    nh_ref[:, 0, :] = h_last
    yb_ref[...] = (_gelu(bg) * u_ref[...].reshape(tm, CT)).astype(yb_ref.dtype)


def _proj_b(h, w_in, l, ls, state, h0, wdw, bdw, wr, br, wi, bi, lam, nb, t):
    m, d = h.shape
    tm = nb * t
    nseq = state.shape[1]
    nj = W_BR // CT
    hpt = CT // HEAD
    vec = pl.BlockSpec((None, 1, CT), lambda i, j: (l, 0, j))
    return pl.pallas_call(
        functools.partial(_proj_b_body, nb=nb, t=t),
        grid=(m // tm, nj),
        in_specs=[_once((tm, d), lambda i, j: (i, 0)),
                  pl.BlockSpec((None, d, CT), lambda i, j: (l, 0, 2 * nj + j)),
                  pl.BlockSpec((None, d, CT), lambda i, j: (l, 0, 3 * nj + j)),
                  pl.BlockSpec((None, nb, K_B - 1, CT), lambda i, j: (ls, i, 0, j)),
                  pl.BlockSpec((None, nb, 1, CT), lambda i, j: (ls, i, 0, j)),
                  pl.BlockSpec((None, K_B, CT), lambda i, j: (l, 0, j)),
                  vec,
                  pl.BlockSpec((None, hpt, HEAD, HEAD), lambda i, j: (l, j, 0, 0)),
                  vec,
                  pl.BlockSpec((None, hpt, HEAD, HEAD), lambda i, j: (l, j, 0, 0)),
                  vec,
                  vec],
        out_specs=[pl.BlockSpec((tm, CT), lambda i, j: (i, j)),
                   pl.BlockSpec((nb, K_B - 1, CT), lambda i, j: (i, 0, j)),
                   pl.BlockSpec((nb, 1, CT), lambda i, j: (i, 0, j))],
        out_shape=[jax.ShapeDtypeStruct((m, W_BR), BF16),
                   jax.ShapeDtypeStruct((nseq, K_B - 1, W_BR), F32),
                   jax.ShapeDtypeStruct((nseq, 1, W_BR), F32)],
        scratch_shapes=[pltpu.VMEM((d, 2 * CT), BF16),
                        pltpu.VMEM((nb, OFF_S + t, CT), F32),
                        pltpu.VMEM((nb, t, CT), F32),
                        pltpu.VMEM((nb, t, CT), F32)],
        compiler_params=_params(("parallel", "arbitrary")),
        name="proj_b",
    )(h, w_in, w_in, state, h0, wdw, bdw, wr, br, wi, bi, lam)


def _merge_body(uc_ref, yb_ref, ga_ref, gb_ref, wa_ref, wb_ref, lng_ref, lnb_ref, o_ref,
                ya_ref, wa_scr, wb_scr):
    @pl.when(pl.program_id(1) == 0)
    def _():
        x = uc_ref[...]
        mu = jnp.mean(x, axis=-1, keepdims=True)
        xc = x - mu
        var = jnp.mean(xc * xc, axis=-1, keepdims=True)
        y = xc * lax.rsqrt(var + EPS) * lng_ref[...] + lnb_ref[...]
        ya_ref[...] = (y * _sigmoid(y)).astype(ya_ref.dtype)

    wa_scr[...] = wa_ref[...].astype(BF16)
    wb_scr[...] = wb_ref[...].astype(BF16)
    out_a = _dot(ya_ref[...], wa_scr[...])
    out_b = _dot(yb_ref[...], wb_scr[...])
    mixed = ga_ref[...].astype(F32) * out_a + gb_ref[...].astype(F32) * out_b
    o_ref[...] = mixed.astype(o_ref.dtype)


def _merge(uc, yb, gates, wa, wb, ln_g, ln_b, l, tm):
    m = uc.shape[0]
    nj = D_MODEL // TN
    return pl.pallas_call(
        _merge_body,
        grid=(m // tm, nj),
        in_specs=[_once((tm, W_BR), lambda i, j: (i, 0)),
                  _once((tm, W_BR), lambda i, j: (i, 0)),
                  pl.BlockSpec((tm, TN), lambda i, j: (i, j)),
                  pl.BlockSpec((tm, TN), lambda i, j: (i, nj + j)),
                  pl.BlockSpec((None, W_BR, TN), lambda i, j: (l, 0, j)),
                  pl.BlockSpec((None, W_BR, TN), lambda i, j: (l, 0, j)),
                  pl.BlockSpec((None, 1, W_BR), lambda i, j: (l, 0, 0)),
                  pl.BlockSpec((None, 1, W_BR), lambda i, j: (l, 0, 0))],
        out_specs=pl.BlockSpec((tm, TN), lambda i, j: (i, j)),
        out_shape=jax.ShapeDtypeStruct((m, D_MODEL), BF16),
        scratch_shapes=[pltpu.VMEM((tm, W_BR), BF16),
                        pltpu.VMEM((W_BR, TN), BF16),
                        pltpu.VMEM((W_BR, TN), BF16)],
        compiler_params=_params(("parallel", "arbitrary")),
        name="merge",
    )(uc, yb, gates, gates, wa, wb, ln_g, ln_b)


def _ffn_up_body(h_ref, w1_ref, w2_ref, st1_ref, st2_ref, wd1_ref, wd2_ref, bd1_ref, bd2_ref,
                 g_ref, ns1_ref, ns2_ref, w_scr, xp_ref, *, nb, t):
    tm = nb * t
    j = pl.program_id(1)

    @pl.when(j < NJ_FF)
    def _():
        _cast_pair(w_scr, w1_ref, w2_ref)
        r = _dot(h_ref[...], w_scr[...])
        halves = ((st1_ref, wd1_ref, bd1_ref, ns1_ref), (st2_ref, wd2_ref, bd2_ref, ns2_ref))
        conv = []
        for n, (st_ref, wd_ref, bd_ref, ns_ref) in enumerate(halves):
            _fill_conv_scratch(xp_ref, r[:, n * CT:(n + 1) * CT], st_ref, K_F, OFF_S, nb, t)
            ns_ref[...] = xp_ref[:, OFF_S + t - (K_F - 1):OFF_S + t, :]
            conv.append(_conv_small(xp_ref, wd_ref, bd_ref, K_F, OFF_S, t).reshape(tm, CT))
        g_ref[...] = (_gelu(conv[0]) * conv[1]).astype(g_ref.dtype)

    @pl.when(j >= NJ_FF)
    def _():
        g_ref[...] = jnp.zeros(g_ref.shape, g_ref.dtype)


def _ffn_up(h, w_up, l, ls, state, wdw, bdw, nb, t):
    m, d = h.shape
    tm = nb * t
    nseq = state.shape[1]
    last = NJ_FF - 1

    def lo(j):
        return jnp.minimum(j, last)

    def hi(j):
        return NJ_FF + jnp.minimum(j, last)

    st_shape = jax.ShapeDtypeStruct((nseq, K_F - 1, D_FF), F32)
    return pl.pallas_call(
        functools.partial(_ffn_up_body, nb=nb, t=t),
        grid=(m // tm, D_FF_PAD // CT),
        in_specs=[_once((tm, d), lambda i, j: (i, 0)),
                  pl.BlockSpec((None, d, CT), lambda i, j: (l, 0, lo(j))),
                  pl.BlockSpec((None, d, CT), lambda i, j: (l, 0, hi(j))),
                  pl.BlockSpec((None, nb, K_F - 1, CT), lambda i, j: (ls, i, 0, lo(j))),
                  pl.BlockSpec((None, nb, K_F - 1, CT), lambda i, j: (ls, i, 0, hi(j))),
                  pl.BlockSpec((None, K_F, CT), lambda i, j: (l, 0, lo(j))),
                  pl.BlockSpec((None, K_F, CT), lambda i, j: (l, 0, hi(j))),
                  pl.BlockSpec((None, 1, CT), lambda i, j: (l, 0, lo(j))),
                  pl.BlockSpec((None, 1, CT), lambda i, j: (l, 0, hi(j)))],
        out_specs=[pl.BlockSpec((tm, CT), lambda i, j: (i, j)),
                   pl.BlockSpec((nb, K_F - 1, CT), lambda i, j: (i, 0, lo(j))),
                   pl.BlockSpec((nb, K_F - 1, CT), lambda i, j: (i, 0, lo(j)))],
        out_shape=[jax.ShapeDtypeStruct((m, D_FF_PAD), BF16), st_shape, st_shape],
        scratch_shapes=[pltpu.VMEM((d, 2 * CT), BF16),
                        pltpu.VMEM((nb, OFF_S + t, CT), F32)],
        compiler_params=_params(("parallel", "arbitrary")),
        name="ffn_up",
    )(h, w_up, w_up, state, state, wdw, wdw, bdw, bdw)


def _layer(x3, states, ls, w, l, nb, tm_merge):
    s_a, s_b, s_h, s_f = states
    bsz, t, d = x3.shape
    m = bsz * t
    tm = nb * t
    x = x3.reshape(m, d)

    h = _rms_cast(x, w["g_pre_mix"], l, 256)
    uc, new_sa = _proj_a(h, w["w_in"], l, ls, s_a, w["w_dw_a"], w["b_dw_a"], nb, t)
    yb, new_sb, new_h = _proj_b(h, w["w_in"], l, ls, s_b, s_h, w["w_dw_b"], w["b_dw_b"],
                                w["w_rg_r"], w["b_rg_r"], w["w_rg_i"], w["b_rg_i"],
                                w["lru_lambda"], nb, t)
    gates = _proj(h, w["w_in"], l, 4 * W_BR, 2 * D_MODEL, tm, _sigmoid, BF16, "proj_gates")
    mixed = _merge(uc, yb, gates, w["w_a_out"], w["w_b_out"], w["ln_a_g"], w["ln_a_b"], l, tm_merge)
    p = _proj(mixed, w["w_o"], l, 0, D_MODEL, tm, None, F32, "proj_o")
    x1, h2 = _resid_rms(x, p, w["g_post_mix"], w["g_pre_ffn"], l, 256)
    g, ns1, ns2 = _ffn_up(h2, w["w_up"], l, ls, s_f, w["w_dw_f"], w["b_dw_f"], nb, t)
    f = _ffn_down(g, w["w_down"], l, tm, 1024)
    y = _resid_rms(x1, f, w["g_post_ffn"], None, l, 256)
    return (y.reshape(bsz, t, d), new_sa, new_sb, new_h[:, 0, :],
            jnp.concatenate([ns1, ns2], axis=-1))


def kernel(x_prompt, x_sample, state_conv_a, state_conv_b, state_lru, state_ffn, g_pre_mix, w_in, w_dw_a, b_dw_a, ln_a_g, ln_a_b, w_a_out, w_dw_b, b_dw_b, w_rg_r, b_rg_r, w_rg_i, b_rg_i, lru_lambda, w_b_out, w_o, g_post_mix, g_pre_ffn, w_up, w_dw_f, b_dw_f, w_down, g_post_ffn):
    depth = w_in.shape[0]
    bp, tp, _ = x_prompt.shape
    bs, ts, _ = x_sample.shape
    dt = x_prompt.dtype
    w = dict(w_in=w_in, w_dw_a=w_dw_a, w_a_out=w_a_out, w_dw_b=w_dw_b, w_rg_r=w_rg_r,
             w_rg_i=w_rg_i, w_b_out=w_b_out, w_o=w_o, w_up=w_up, w_dw_f=w_dw_f, w_down=w_down)
    vecs = dict(g_pre_mix=g_pre_mix, b_dw_a=b_dw_a, ln_a_g=ln_a_g, ln_a_b=ln_a_b, b_dw_b=b_dw_b,
                b_rg_r=b_rg_r, b_rg_i=b_rg_i, lru_lambda=lru_lambda, g_post_mix=g_post_mix,
                g_pre_ffn=g_pre_ffn, b_dw_f=b_dw_f, g_post_ffn=g_post_ffn)
    w.update({name: v[:, None, :] for name, v in vecs.items()})
    zeros_p = (jnp.zeros((1, bp, K_A - 1, W_BR), dt), jnp.zeros((1, bp, K_B - 1, W_BR), dt),
               jnp.zeros((1, bp, 1, W_BR), dt), jnp.zeros((1, bp, K_F - 1, 2 * D_FF), dt))
    state_s = (state_conv_a, state_conv_b, state_lru[:, :, None, :], state_ffn)
    yp, ys = x_prompt, x_sample
    outs_p = [[], [], [], []]
    outs_s = [[], [], [], []]
    for l in range(depth):
        yp, *new_p = _layer(yp, zeros_p, 0, w, l, 1, 1024)
        ys, *new_s = _layer(ys, state_s, l, w, l, bs, bs * ts)
        for acc, new in ((outs_p, new_p), (outs_s, new_s)):
            for lst, v in zip(acc, new):
                lst.append(v)
    return (yp, ys, *[jnp.stack(v) for v in outs_p], *[jnp.stack(v) for v in outs_s])
```

```python
import dataclasses
import functools
import math

import jax
import jax.numpy as jnp
from jax import lax
from jax.experimental import pallas as pl
from jax.experimental.pallas import tpu as pltpu

F32 = jnp.float32
BF16 = jnp.bfloat16

D_MODEL = 4096
W_BR = D_MODEL // 2
N_HEADS = 16
HEAD = W_BR // N_HEADS
K_A = 31
K_B = 4
K_F = 3
D_FF = 11008
LRU_C = 8.0
EPS = 1e-6

V7X_VMEM_LIMIT = 58 * 1024 * 1024
SUBLANES = 8
CT = 256
NJ_FF = D_FF // CT
TK_DOWN = 1024
CONV_ROWS = 32
TN = 512
TM_ROWWISE = 256


@dataclasses.dataclass(frozen=True)
class Tile:
    q: int
    h: int
    phased: bool

    @property
    def rows(self):
        return self.q * self.h


def _params(sem):
    return pltpu.CompilerParams(dimension_semantics=sem, vmem_limit_bytes=V7X_VMEM_LIMIT)


def _dot(a, b):
    return jnp.dot(a, b, preferred_element_type=F32)


def _sigmoid(x):
    return 1.0 / (1.0 + jnp.exp(-x))


def _gelu(x):
    c = math.sqrt(2.0 / math.pi)
    return x * (0.5 * (1.0 + jnp.tanh(c * (x + 0.044715 * (x * x * x)))))


def _once(shape, index_map):
    return pl.BlockSpec(shape, index_map, pipeline_mode=pl.Buffered(1))


def _cast_pair(w_scr, wa_ref, wb_ref):
    n = wa_ref.shape[-1]
    w_scr[:, :n] = wa_ref[...].astype(BF16)
    w_scr[:, n:] = wb_ref[...].astype(BF16)


def _rms_cast_body(x_ref, g_ref, o_ref):
    x = x_ref[...]
    y = x * lax.rsqrt(jnp.mean(x * x, axis=-1, keepdims=True) + EPS)
    o_ref[...] = (y * g_ref[...]).astype(o_ref.dtype)


def _rms_cast(x, g, l):
    m, d = x.shape
    tm = TM_ROWWISE
    return pl.pallas_call(
        _rms_cast_body,
        grid=(m // tm,),
        in_specs=[pl.BlockSpec((tm, d), lambda i: (i, 0)),
                  pl.BlockSpec((None, 1, d), lambda i: (l, 0, 0))],
        out_specs=pl.BlockSpec((tm, d), lambda i: (i, 0)),
        out_shape=jax.ShapeDtypeStruct((m, d), BF16),
        compiler_params=_params(("parallel",)),
        name="rms_cast",
    )(x, g)


def _resid_rms_body(x_ref, p_ref, g_ref, o_ref):
    p = p_ref[...]
    y = p * lax.rsqrt(jnp.mean(p * p, axis=-1, keepdims=True) + EPS)
    o_ref[...] = x_ref[...] + y * g_ref[...]


def _resid_rms_next_body(x_ref, p_ref, g_ref, gn_ref, o_ref, h_ref):
    _resid_rms_body(x_ref, p_ref, g_ref, o_ref)
    x1 = o_ref[...]
    h = x1 * lax.rsqrt(jnp.mean(x1 * x1, axis=-1, keepdims=True) + EPS)
    h_ref[...] = (h * gn_ref[...]).astype(h_ref.dtype)


def _resid_rms(x, p, g, g_next, l):
    m, d = x.shape
    tm = TM_ROWWISE
    row = pl.BlockSpec((tm, d), lambda i: (i, 0))
    vec = pl.BlockSpec((None, 1, d), lambda i: (l, 0, 0))
    if g_next is None:
        return pl.pallas_call(
            _resid_rms_body, grid=(m // tm,), in_specs=[row, row, vec], out_specs=row,
            out_shape=jax.ShapeDtypeStruct((m, d), F32),
            compiler_params=_params(("parallel",)), name="resid_rms",
        )(x, p, g)
    return pl.pallas_call(
        _resid_rms_next_body, grid=(m // tm,), in_specs=[row, row, vec, vec], out_specs=[row, row],
        out_shape=[jax.ShapeDtypeStruct((m, d), F32), jax.ShapeDtypeStruct((m, d), BF16)],
        compiler_params=_params(("parallel",)), name="resid_rms_next",
    )(x, p, g, g_next)


def _proj_body(a_ref, w_ref, o_ref, w_scr, *, act):
    w_scr[...] = w_ref[...].astype(BF16)
    r = _dot(a_ref[...], w_scr[...])
    if act is not None:
        r = act(r)
    o_ref[...] = r.astype(o_ref.dtype)


def _proj(a, w, l, col0, ncols, tm, act, out_dtype, name):
    m, d = a.shape
    jb = col0 // TN
    return pl.pallas_call(
        functools.partial(_proj_body, act=act),
        grid=(m // tm, ncols // TN),
        in_specs=[_once((tm, d), lambda i, j: (i, 0)),
                  pl.BlockSpec((None, d, TN), lambda i, j: (l, 0, jb + j))],
        out_specs=pl.BlockSpec((tm, TN), lambda i, j: (i, j)),
        out_shape=jax.ShapeDtypeStruct((m, ncols), out_dtype),
        scratch_shapes=[pltpu.VMEM((d, TN), BF16)],
        compiler_params=_params(("parallel", "arbitrary")),
        name=name,
    )(a, w)


def _down_body(a_ref, w_ref, o_ref, w_scr, *, k_rows):
    k = pl.program_id(2)

    @pl.when(k == 0)
    def _():
        o_ref[...] = jnp.zeros(o_ref.shape, o_ref.dtype)

    limit = k_rows - k * TK_DOWN
    row = lax.broadcasted_iota(jnp.int32, w_ref.shape, 0)
    col = lax.broadcasted_iota(jnp.int32, (1, TK_DOWN), 1)
    w_scr[...] = jnp.where(row < limit, w_ref[...], 0.0).astype(BF16)
    a = jnp.where(col < limit, a_ref[...], jnp.zeros((), a_ref.dtype))
    o_ref[...] += _dot(a, w_scr[...])


def _ffn_down(g, w, l, tm, tn):
    m = g.shape[0]
    kd, n = w.shape[1:]
    nk = pl.cdiv(kd, TK_DOWN)
    return pl.pallas_call(
        functools.partial(_down_body, k_rows=kd),
        grid=(m // tm, n // tn, nk),
        in_specs=[pl.BlockSpec((tm, TK_DOWN), lambda i, j, k: (i, k)),
                  pl.BlockSpec((None, TK_DOWN, tn), lambda i, j, k: (l, k, j))],
        out_specs=pl.BlockSpec((tm, tn), lambda i, j, k: (i, j)),
        out_shape=jax.ShapeDtypeStruct((m, n), F32),
        scratch_shapes=[pltpu.VMEM((TK_DOWN, tn), BF16)],
        compiler_params=_params(("parallel", "parallel", "arbitrary")),
        name="ffn_down",
    )(g, w)


def _state_in_spec(tile, k1, ls):
    if tile.phased:
        return lambda col: pl.BlockSpec((None, 1, k1, CT), lambda i, j: (ls, i, 0, col(j)))
    return lambda col: pl.BlockSpec((None, k1, tile.h, CT), lambda i, j: (ls, 0, 0, col(j)))


def _state_out(tile, k1, nseq, width):
    if tile.phased:
        spec = lambda col: pl.BlockSpec((1, k1, CT), lambda i, j: (i, 0, col(j)))
        return spec, jax.ShapeDtypeStruct((nseq, k1, width), F32)
    spec = lambda col: pl.BlockSpec((k1, tile.h, CT), lambda i, j: (0, 0, col(j)))
    return spec, jax.ShapeDtypeStruct((k1, nseq, width), F32)


def _prev_slabs(tile, x, st_ref, k1):
    if not tile.phased:
        return st_ref[...].reshape(k1 * tile.h, x.shape[-1])
    sub = lax.broadcasted_iota(jnp.int32, (tile.h, x.shape[-1]), 0)
    out = []
    for j in range(k1):
        slab = x[(tile.q - k1 + j) * tile.h:(tile.q - k1 + j + 1) * tile.h]
        out.append(jnp.where(sub == 0, st_ref[0, j:j + 1, :], pltpu.roll(slab, 1, axis=0)))
    return jnp.concatenate(out, axis=0)


def _store_new_state(tile, ns_ref, tail, k1):
    tail3 = tail.reshape(k1, tile.h, tail.shape[-1])
    if tile.phased:
        ns_ref[0] = tail3[:, tile.h - 1, :]
    else:
        ns_ref[...] = tail3


def _conv_value(tile, x, st_ref, ns_ref, w_ref, b_ref, k_taps):
    k1 = k_taps - 1
    xp = jnp.concatenate([_prev_slabs(tile, x, st_ref, k1), x], axis=0)
    _store_new_state(tile, ns_ref, xp[tile.rows:], k1)
    acc = b_ref[...] + w_ref[0:1, :] * xp[0:tile.rows]
    for k in range(1, k_taps):
        acc = acc + w_ref[k:k + 1, :] * xp[k * tile.h:k * tile.h + tile.rows]
    return acc


def _proj_a_body(h_ref, wv_ref, wg_ref, st_ref, wdw_ref, bdw_ref, uc_ref, ns_ref, w_scr, xa_ref,
                 xb_ref, wb_ref, *, tile):
    k1 = K_A - 1
    head = k1 * tile.h
    s = pl.program_id(0)
    rows = min(CONV_ROWS, tile.rows)
    groups = rows // SUBLANES

    @pl.when(s == 0)
    def _():
        xb_ref[...] = jnp.zeros(xb_ref.shape, xb_ref.dtype)

    def step(xp_cur, xp_prev):
        for k in range(K_A):
            wb_ref[k] = jnp.broadcast_to(wdw_ref[k:k + 1, :], (SUBLANES, CT))
        bias = jnp.broadcast_to(bdw_ref[...], (groups, SUBLANES, CT))
        for r0 in range(0, tile.rows, rows):
            acc = bias
            for k in range(K_A):
                x = xp_prev[r0 + k * tile.h:r0 + k * tile.h + rows, :]
                acc = acc + wb_ref[k][None] * x.reshape(groups, SUBLANES, CT)
            uc_ref[r0:r0 + rows, :] = acc.reshape(rows, CT)

        _cast_pair(w_scr, wv_ref, wg_ref)
        r = _dot(h_ref[...], w_scr[...])
        ua = r[:, :CT] * _sigmoid(r[:, CT:])
        xp_cur[0:head, :] = _prev_slabs(tile, ua, st_ref, k1)
        xp_cur[head:, :] = ua
        _store_new_state(tile, ns_ref, xp_cur[tile.rows:, :], k1)

    @pl.when(s % 2 == 0)
    def _():
        step(xa_ref, xb_ref)

    @pl.when(s % 2 == 1)
    def _():
        step(xb_ref, xa_ref)


def _skewed(ni, nj):
    last = ni * nj - 1

    def cur(s):
        s = jnp.minimum(s, last)
        return s // nj, s % nj

    def prev(s):
        s = jnp.maximum(s - 1, 0)
        return s // nj, s % nj

    return cur, prev


def _proj_a(h, w_in, l, ls, state, wdw, bdw, tile, nseq):
    m, d = h.shape
    tm = tile.rows
    nj = W_BR // CT
    ni = m // tm
    k1 = K_A - 1
    cur, prev = _skewed(ni, nj)
    if tile.phased:
        st_spec = pl.BlockSpec((None, 1, k1, CT), lambda s: (ls, cur(s)[0], 0, cur(s)[1]))
        ns_spec = pl.BlockSpec((1, k1, CT), lambda s: (cur(s)[0], 0, cur(s)[1]))
        ns_shape = jax.ShapeDtypeStruct((nseq, k1, W_BR), F32)
    else:
        st_spec = pl.BlockSpec((None, k1, tile.h, CT), lambda s: (ls, 0, 0, cur(s)[1]))
        ns_spec = pl.BlockSpec((k1, tile.h, CT), lambda s: (0, 0, cur(s)[1]))
        ns_shape = jax.ShapeDtypeStruct((k1, nseq, W_BR), F32)
    xp_shape = pltpu.VMEM((k1 * tile.h + tm, CT), F32)
    return pl.pallas_call(
        functools.partial(_proj_a_body, tile=tile),
        grid=(ni * nj + 1,),
        in_specs=[_once((tm, d), lambda s: (cur(s)[0], 0)),
                  pl.BlockSpec((None, d, CT), lambda s: (l, 0, cur(s)[1])),
                  pl.BlockSpec((None, d, CT), lambda s: (l, 0, nj + cur(s)[1])),
                  st_spec,
                  pl.BlockSpec((None, K_A, CT), lambda s: (l, 0, prev(s)[1])),
                  pl.BlockSpec((None, 1, CT), lambda s: (l, 0, prev(s)[1]))],
        out_specs=[pl.BlockSpec((tm, CT), lambda s: prev(s)), ns_spec],
        out_shape=[jax.ShapeDtypeStruct((m, W_BR), F32), ns_shape],
        scratch_shapes=[pltpu.VMEM((d, 2 * CT), BF16), xp_shape, xp_shape,
                        pltpu.VMEM((K_A, SUBLANES, CT), F32)],
        compiler_params=_params(("arbitrary",)),
        name="proj_a",
    )(h, w_in, w_in, state, wdw, bdw)


def _lru_scan(tile, a_ref, u_ref, h0):
    hq = tile.h
    if not tile.phased:
        hcur = h0
        for q in range(tile.q):
            hcur = a_ref[q * hq:(q + 1) * hq, :] * hcur + u_ref[q * hq:(q + 1) * hq, :]
            u_ref[q * hq:(q + 1) * hq, :] = hcur
        return hcur

    def slab(q, carry):
        acum, z = carry
        r0 = pl.multiple_of(q * hq, hq)
        a = a_ref[pl.ds(r0, hq), :]
        acum = a * acum
        z = a * z + u_ref[pl.ds(r0, hq), :]
        a_ref[pl.ds(r0, hq), :] = acum
        u_ref[pl.ds(r0, hq), :] = z
        return acum, z

    ones = jnp.ones((hq, a_ref.shape[-1]), F32)
    a_end, z_end = lax.fori_loop(0, tile.q, slab, (ones, jnp.zeros_like(ones)), unroll=8)
    starts = []
    hcur = h0
    for p in range(hq):
        starts.append(hcur)
        hcur = a_end[p:p + 1, :] * hcur + z_end[p:p + 1, :]
    start = jnp.concatenate(starts, axis=0)
    c = a_ref.shape[-1]
    hs = a_ref[...].reshape(tile.q, hq, c) * start[None] + u_ref[...].reshape(tile.q, hq, c)
    u_ref[...] = hs.reshape(tile.rows, c)
    return hcur


def _proj_b_body(h_ref, wx_ref, wg_ref, st_ref, h0_ref, wdw_ref, bdw_ref, wr_ref, br_ref, wi_ref,
                 bi_ref, lam_ref, yb_ref, ns_ref, nh_ref, w_scr, a_ref, u_ref, *, tile):
    _cast_pair(w_scr, wx_ref, wg_ref)
    r = _dot(h_ref[...], w_scr[...])
    bg = r[:, CT:]
    xb = _conv_value(tile, r[:, :CT], st_ref, ns_ref, wdw_ref, bdw_ref, K_B)

    xb16 = xb.astype(BF16)
    heads = range(CT // HEAD)
    r_lin = jnp.concatenate(
        [_dot(xb16[:, hd * HEAD:(hd + 1) * HEAD], wr_ref[hd].astype(BF16)) for hd in heads], axis=1)
    i_lin = jnp.concatenate(
        [_dot(xb16[:, hd * HEAD:(hd + 1) * HEAD], wi_ref[hd].astype(BF16)) for hd in heads], axis=1)
    rg = _sigmoid(r_lin + br_ref[...])
    ig = _sigmoid(i_lin + bi_ref[...])
    neg_lam = -lam_ref[...]
    softplus = jnp.maximum(neg_lam, 0.0) + jnp.log1p(jnp.exp(-jnp.abs(neg_lam)))
    log_a = (-LRU_C) * rg * softplus
    a_ref[...] = jnp.exp(log_a)
    th = jnp.tanh(log_a)
    u_ref[...] = jnp.sqrt(-2.0 * th / (1.0 - th)) * (ig * xb)

    h_last = _lru_scan(tile, a_ref, u_ref, h0_ref[...].reshape(-1, CT))
    nh_ref[...] = h_last.reshape(nh_ref.shape)
    yb_ref[...] = (_gelu(bg) * u_ref[...]).astype(yb_ref.dtype)


def _proj_b(h, w_in, l, ls, state, h0, wdw, bdw, wr, br, wi, bi, lam, tile, nseq):
    m, d = h.shape
    tm = tile.rows
    nj = W_BR // CT
    hpt = CT // HEAD
    k1 = K_B - 1
    col = lambda j: j
    vec = pl.BlockSpec((None, 1, CT), lambda i, j: (l, 0, j))
    ns_spec, ns_shape = _state_out(tile, k1, nseq, W_BR)
    if tile.phased:
        h0_spec = pl.BlockSpec((None, 1, 1, CT), lambda i, j: (ls, i, 0, j))
        nh_spec = pl.BlockSpec((1, 1, CT), lambda i, j: (i, 0, j))
        nh_shape = jax.ShapeDtypeStruct((nseq, 1, W_BR), F32)
    else:
        h0_spec = pl.BlockSpec((None, tile.h, CT), lambda i, j: (ls, 0, j))
        nh_spec = pl.BlockSpec((tile.h, CT), lambda i, j: (0, j))
        nh_shape = jax.ShapeDtypeStruct((nseq, W_BR), F32)
    return pl.pallas_call(
        functools.partial(_proj_b_body, tile=tile),
        grid=(m // tm, nj),
        in_specs=[_once((tm, d), lambda i, j: (i, 0)),
                  pl.BlockSpec((None, d, CT), lambda i, j: (l, 0, 2 * nj + j)),
                  pl.BlockSpec((None, d, CT), lambda i, j: (l, 0, 3 * nj + j)),
                  _state_in_spec(tile, k1, ls)(col),
                  h0_spec,
                  pl.BlockSpec((None, K_B, CT), lambda i, j: (l, 0, j)),
                  vec,
                  pl.BlockSpec((None, hpt, HEAD, HEAD), lambda i, j: (l, j, 0, 0)),
                  vec,
                  pl.BlockSpec((None, hpt, HEAD, HEAD), lambda i, j: (l, j, 0, 0)),
                  vec,
                  vec],
        out_specs=[pl.BlockSpec((tm, CT), lambda i, j: (i, j)), ns_spec(col), nh_spec],
        out_shape=[jax.ShapeDtypeStruct((m, W_BR), BF16), ns_shape, nh_shape],
        scratch_shapes=[pltpu.VMEM((d, 2 * CT), BF16),
                        pltpu.VMEM((tm, CT), F32),
                        pltpu.VMEM((tm, CT), F32)],
        compiler_params=_params(("parallel", "arbitrary")),
        name="proj_b",
    )(h, w_in, w_in, state, h0, wdw, bdw, wr, br, wi, bi, lam)


def _merge_body(uc_ref, yb_ref, ga_ref, gb_ref, wa_ref, wb_ref, lng_ref, lnb_ref, o_ref,
                ya_ref, wa_scr, wb_scr):
    @pl.when(pl.program_id(1) == 0)
    def _():
        x = uc_ref[...]
        mu = jnp.mean(x, axis=-1, keepdims=True)
        xc = x - mu
        var = jnp.mean(xc * xc, axis=-1, keepdims=True)
        y = xc * lax.rsqrt(var + EPS) * lng_ref[...] + lnb_ref[...]
        ya_ref[...] = (y * _sigmoid(y)).astype(ya_ref.dtype)

    wa_scr[...] = wa_ref[...].astype(BF16)
    wb_scr[...] = wb_ref[...].astype(BF16)
    out_a = _dot(ya_ref[...], wa_scr[...])
    out_b = _dot(yb_ref[...], wb_scr[...])
    mixed = ga_ref[...].astype(F32) * out_a + gb_ref[...].astype(F32) * out_b
    o_ref[...] = mixed.astype(o_ref.dtype)


def _merge(uc, yb, gates, wa, wb, ln_g, ln_b, l, tm):
    m = uc.shape[0]
    nj = D_MODEL // TN
    return pl.pallas_call(
        _merge_body,
        grid=(m // tm, nj),
        in_specs=[_once((tm, W_BR), lambda i, j: (i, 0)),
                  _once((tm, W_BR), lambda i, j: (i, 0)),
                  pl.BlockSpec((tm, TN), lambda i, j: (i, j)),
                  pl.BlockSpec((tm, TN), lambda i, j: (i, nj + j)),
                  pl.BlockSpec((None, W_BR, TN), lambda i, j: (l, 0, j)),
                  pl.BlockSpec((None, W_BR, TN), lambda i, j: (l, 0, j)),
                  pl.BlockSpec((None, 1, W_BR), lambda i, j: (l, 0, 0)),
                  pl.BlockSpec((None, 1, W_BR), lambda i, j: (l, 0, 0))],
        out_specs=pl.BlockSpec((tm, TN), lambda i, j: (i, j)),
        out_shape=jax.ShapeDtypeStruct((m, D_MODEL), BF16),
        scratch_shapes=[pltpu.VMEM((tm, W_BR), BF16),
                        pltpu.VMEM((W_BR, TN), BF16),
                        pltpu.VMEM((W_BR, TN), BF16)],
        compiler_params=_params(("parallel", "arbitrary")),
        name="merge",
    )(uc, yb, gates, gates, wa, wb, ln_g, ln_b)


def _ffn_up_body(h_ref, w1_ref, w2_ref, st1_ref, st2_ref, wd1_ref, wd2_ref, bd1_ref, bd2_ref,
                 g_ref, ns1_ref, ns2_ref, w_scr, *, tile):
    _cast_pair(w_scr, w1_ref, w2_ref)
    r = _dot(h_ref[...], w_scr[...])
    c1 = _conv_value(tile, r[:, :CT], st1_ref, ns1_ref, wd1_ref, bd1_ref, K_F)
    c2 = _conv_value(tile, r[:, CT:], st2_ref, ns2_ref, wd2_ref, bd2_ref, K_F)
    g_ref[...] = (_gelu(c1) * c2).astype(g_ref.dtype)


def _ffn_up(h, w_up, l, ls, state, wdw, bdw, tile, nseq):
    m, d = h.shape
    tm = tile.rows
    k1 = K_F - 1
    lo = lambda j: j
    hi = lambda j: NJ_FF + j
    st_spec = _state_in_spec(tile, k1, ls)
    ns_spec, ns_shape = _state_out(tile, k1, nseq, D_FF)
    return pl.pallas_call(
        functools.partial(_ffn_up_body, tile=tile),
        grid=(m // tm, NJ_FF),
        in_specs=[_once((tm, d), lambda i, j: (i, 0)),
                  pl.BlockSpec((None, d, CT), lambda i, j: (l, 0, lo(j))),
                  pl.BlockSpec((None, d, CT), lambda i, j: (l, 0, hi(j))),
                  st_spec(lo), st_spec(hi),
                  pl.BlockSpec((None, K_F, CT), lambda i, j: (l, 0, lo(j))),
                  pl.BlockSpec((None, K_F, CT), lambda i, j: (l, 0, hi(j))),
                  pl.BlockSpec((None, 1, CT), lambda i, j: (l, 0, lo(j))),
                  pl.BlockSpec((None, 1, CT), lambda i, j: (l, 0, hi(j)))],
        out_specs=[pl.BlockSpec((tm, CT), lambda i, j: (i, j)), ns_spec(lo), ns_spec(lo)],
        out_shape=[jax.ShapeDtypeStruct((m, D_FF), BF16), ns_shape, ns_shape],
        scratch_shapes=[pltpu.VMEM((d, 2 * CT), BF16)],
        compiler_params=_params(("parallel", "arbitrary")),
        name="ffn_up",
    )(h, w_up, w_up, state, state, wdw, wdw, bdw, bdw)


def _layer(x, states, ls, w, l, tile, nseq, tm_merge):
    s_a, s_b, s_h, s_f = states
    tm = tile.rows
    h = _rms_cast(x, w["g_pre_mix"], l)
    uc, new_sa = _proj_a(h, w["w_in"], l, ls, s_a, w["w_dw_a"], w["b_dw_a"], tile, nseq)
    yb, new_sb, new_h = _proj_b(h, w["w_in"], l, ls, s_b, s_h, w["w_dw_b"], w["b_dw_b"],
                                w["w_rg_r"], w["b_rg_r"], w["w_rg_i"], w["b_rg_i"],
                                w["lru_lambda"], tile, nseq)
    gates = _proj(h, w["w_in"], l, 4 * W_BR, 2 * D_MODEL, tm, _sigmoid, BF16, "proj_gates")
    mixed = _merge(uc, yb, gates, w["w_a_out"], w["w_b_out"], w["ln_a_g"], w["ln_a_b"], l, tm_merge)
    p = _proj(mixed, w["w_o"], l, 0, D_MODEL, tm, None, F32, "proj_o")
    x1, h2 = _resid_rms(x, p, w["g_post_mix"], w["g_pre_ffn"], l)
    g, ns1, ns2 = _ffn_up(h2, w["w_up"], l, ls, s_f, w["w_dw_f"], w["b_dw_f"], tile, nseq)
    f = _ffn_down(g, w["w_down"], l, tm, 1024)
    y = _resid_rms(x1, f, w["g_post_ffn"], None, l)
    return y, new_sa, new_sb, new_h, jnp.concatenate([ns1, ns2], axis=-1)


def kernel(x_prompt, x_sample, state_conv_a, state_conv_b, state_lru, state_ffn, g_pre_mix, w_in, w_dw_a, b_dw_a, ln_a_g, ln_a_b, w_a_out, w_dw_b, b_dw_b, w_rg_r, b_rg_r, w_rg_i, b_rg_i, lru_lambda, w_b_out, w_o, g_post_mix, g_pre_ffn, w_up, w_dw_f, b_dw_f, w_down, g_post_ffn):
    depth = w_in.shape[0]
    bp, tp, d = x_prompt.shape
    bs, ts, _ = x_sample.shape
    dt = x_prompt.dtype
    w = dict(w_in=w_in, w_dw_a=w_dw_a, w_a_out=w_a_out, w_dw_b=w_dw_b, w_rg_r=w_rg_r,
             w_rg_i=w_rg_i, w_b_out=w_b_out, w_o=w_o, w_up=w_up, w_dw_f=w_dw_f, w_down=w_down)
    vecs = dict(g_pre_mix=g_pre_mix, b_dw_a=b_dw_a, ln_a_g=ln_a_g, ln_a_b=ln_a_b, b_dw_b=b_dw_b,
                b_rg_r=b_rg_r, b_rg_i=b_rg_i, lru_lambda=lru_lambda, g_post_mix=g_post_mix,
                g_pre_ffn=g_pre_ffn, b_dw_f=b_dw_f, g_post_ffn=g_post_ffn)
    w.update({name: v[:, None, :] for name, v in vecs.items()})

    tile_p = Tile(q=tp // SUBLANES, h=SUBLANES, phased=True)
    yp = x_prompt.reshape(bp, SUBLANES, tile_p.q, d).swapaxes(1, 2).reshape(bp * tp, d)
    zeros_p = (jnp.zeros((1, bp, K_A - 1, W_BR), dt), jnp.zeros((1, bp, K_B - 1, W_BR), dt),
               jnp.zeros((1, bp, 1, W_BR), dt), jnp.zeros((1, bp, K_F - 1, 2 * D_FF), dt))
    tile_s = Tile(q=ts, h=bs, phased=False)
    ys = x_sample.swapaxes(0, 1).reshape(ts * bs, d)
    state_s = (state_conv_a.swapaxes(1, 2), state_conv_b.swapaxes(1, 2), state_lru,
               state_ffn.swapaxes(1, 2))

    outs_p = [[], [], [], []]
    outs_s = [[], [], [], []]
    for l in range(depth):
        yp, *new_p = _layer(yp, zeros_p, 0, w, l, tile_p, bp, 1024)
        ys, *new_s = _layer(ys, state_s, l, w, l, tile_s, bs, tile_s.rows)
        new_p[2] = new_p[2][:, 0, :]
        new_s = [new_s[0].swapaxes(0, 1), new_s[1].swapaxes(0, 1), new_s[2], new_s[3].swapaxes(0, 1)]
        for acc, new in ((outs_p, new_p), (outs_s, new_s)):
            for lst, v in zip(acc, new):
                lst.append(v)
    yp = yp.reshape(bp, tile_p.q, SUBLANES, d).swapaxes(1, 2).reshape(bp, tp, d)
    ys = ys.reshape(ts, bs, d).swapaxes(0, 1)
    return (yp, ys, *[jnp.stack(v) for v in outs_p], *[jnp.stack(v) for v in outs_s])
```

```python
import dataclasses
import functools
import math

import jax
import jax.numpy as jnp
from jax import lax
from jax.experimental import pallas as pl
from jax.experimental.pallas import tpu as pltpu

F32 = jnp.float32
BF16 = jnp.bfloat16

D_MODEL = 4096
W_BR = D_MODEL // 2
N_HEADS = 16
HEAD = W_BR // N_HEADS
K_A = 31
K_B = 4
K_F = 3
D_FF = 11008
LRU_C = 8.0
EPS = 1e-6

V7X_VMEM_LIMIT = 60 * 1024 * 1024
SUBLANES = 8
CT = 256
NJ_BR = W_BR // CT
NJ_FF = D_FF // CT
TK_DOWN = 1024
TN_DOWN = 1024
CONV_ROWS = 32
TN = 512
TM_MERGE = 1024
TM_ROWWISE = 256


@dataclasses.dataclass(frozen=True)
class Tile:
    q: int
    h: int
    phased: bool

    @property
    def rows(self):
        return self.q * self.h


@dataclasses.dataclass(frozen=True)
class Group:
    tile: Tile
    nseq: int
    ls: int


def _params(sem):
    return pltpu.CompilerParams(dimension_semantics=sem, vmem_limit_bytes=V7X_VMEM_LIMIT)


def _dot(a, b):
    return jnp.dot(a, b, preferred_element_type=F32)


def _sigmoid(x):
    return 1.0 / (1.0 + jnp.exp(-x))


def _gelu(x):
    c = math.sqrt(2.0 / math.pi)
    return x * (0.5 * (1.0 + jnp.tanh(c * (x + 0.044715 * (x * x * x)))))


def _once(shape, index_map):
    return pl.BlockSpec(shape, index_map, pipeline_mode=pl.Buffered(1))


def _cast_pair(w_scr, wa_ref, wb_ref):
    n = wa_ref.shape[-1]
    w_scr[:, :n] = wa_ref[...].astype(BF16)
    w_scr[:, n:] = wb_ref[...].astype(BF16)


def _first_tile_only(n):
    return lambda i, j: jnp.where(i == 0, j, n - 1)


def _rms_cast_body(x_ref, g_ref, o_ref):
    x = x_ref[...]
    y = x * lax.rsqrt(jnp.mean(x * x, axis=-1, keepdims=True) + EPS)
    o_ref[...] = (y * g_ref[...]).astype(o_ref.dtype)


def _rms_cast(x, g, l):
    m, d = x.shape
    tm = TM_ROWWISE
    return pl.pallas_call(
        _rms_cast_body,
        grid=(m // tm,),
        in_specs=[pl.BlockSpec((tm, d), lambda i: (i, 0)),
                  pl.BlockSpec((None, 1, d), lambda i: (l, 0, 0))],
        out_specs=pl.BlockSpec((tm, d), lambda i: (i, 0)),
        out_shape=jax.ShapeDtypeStruct((m, d), BF16),
        compiler_params=_params(("parallel",)),
        name="rms_cast",
    )(x, g)


def _resid_rms_body(x_ref, p_ref, g_ref, o_ref):
    p = p_ref[...]
    y = p * lax.rsqrt(jnp.mean(p * p, axis=-1, keepdims=True) + EPS)
    o_ref[...] = x_ref[...] + y * g_ref[...]


def _resid_rms_next_body(x_ref, p_ref, g_ref, gn_ref, o_ref, h_ref):
    _resid_rms_body(x_ref, p_ref, g_ref, o_ref)
    x1 = o_ref[...]
    h = x1 * lax.rsqrt(jnp.mean(x1 * x1, axis=-1, keepdims=True) + EPS)
    h_ref[...] = (h * gn_ref[...]).astype(h_ref.dtype)


def _resid_rms(x, p, g, g_next, l):
    m, d = x.shape
    tm = TM_ROWWISE
    row = pl.BlockSpec((tm, d), lambda i: (i, 0))
    vec = pl.BlockSpec((None, 1, d), lambda i: (l, 0, 0))
    if g_next is None:
        return pl.pallas_call(
            _resid_rms_body, grid=(m // tm,), in_specs=[row, row, vec], out_specs=row,
            out_shape=jax.ShapeDtypeStruct((m, d), F32),
            compiler_params=_params(("parallel",)), name="resid_rms",
        )(x, p, g)
    return pl.pallas_call(
        _resid_rms_next_body, grid=(m // tm,), in_specs=[row, row, vec, vec], out_specs=[row, row],
        out_shape=[jax.ShapeDtypeStruct((m, d), F32), jax.ShapeDtypeStruct((m, d), BF16)],
        compiler_params=_params(("parallel",)), name="resid_rms_next",
    )(x, p, g, g_next)


def _proj_body(ap_ref, as_ref, w_ref, op_ref, os_ref, w_scr, *, act):
    def one(a_ref, o_ref):
        r = _dot(a_ref[...], w_scr[...])
        if act is not None:
            r = act(r)
        o_ref[...] = r.astype(o_ref.dtype)

    w_scr[...] = w_ref[...].astype(BF16)
    one(ap_ref, op_ref)

    @pl.when(pl.program_id(0) == 0)
    def _():
        one(as_ref, os_ref)


def _proj(ap, as_, w, l, col0, ncols, tm, act, out_dtype, name):
    mp, d = ap.shape
    ms = as_.shape[0]
    jb = col0 // TN
    nj = ncols // TN
    sj = _first_tile_only(nj)
    return pl.pallas_call(
        functools.partial(_proj_body, act=act),
        grid=(mp // tm, nj),
        in_specs=[_once((tm, d), lambda i, j: (i, 0)),
                  _once((ms, d), lambda i, j: (0, 0)),
                  pl.BlockSpec((None, d, TN), lambda i, j: (l, 0, jb + j))],
        out_specs=[pl.BlockSpec((tm, TN), lambda i, j: (i, j)),
                   pl.BlockSpec((ms, TN), lambda i, j: (0, sj(i, j)))],
        out_shape=[jax.ShapeDtypeStruct((mp, ncols), out_dtype),
                   jax.ShapeDtypeStruct((ms, ncols), out_dtype)],
        scratch_shapes=[pltpu.VMEM((d, TN), BF16)],
        compiler_params=_params(("arbitrary", "arbitrary")),
        name=name,
    )(ap, as_, w)


def _down_body(ap_ref, as_ref, w_ref, op_ref, os_ref, w_scr, *, k_rows):
    k = pl.program_id(2)
    limit = k_rows - k * TK_DOWN
    row = lax.broadcasted_iota(jnp.int32, w_ref.shape, 0)
    col = lax.broadcasted_iota(jnp.int32, (1, TK_DOWN), 1)

    def one(a_ref, o_ref):
        @pl.when(k == 0)
        def _():
            o_ref[...] = jnp.zeros(o_ref.shape, o_ref.dtype)

        a = jnp.where(col < limit, a_ref[...], jnp.zeros((), a_ref.dtype))
        o_ref[...] += _dot(a, w_scr[...])

    w_scr[...] = jnp.where(row < limit, w_ref[...], 0.0).astype(BF16)
    one(ap_ref, op_ref)

    @pl.when(pl.program_id(0) == 0)
    def _():
        one(as_ref, os_ref)


def _ffn_down(gp, gs, w, l, tm):
    mp = gp.shape[0]
    ms = gs.shape[0]
    kd, n = w.shape[1:]
    nk = pl.cdiv(kd, TK_DOWN)
    nj = n // TN_DOWN
    return pl.pallas_call(
        functools.partial(_down_body, k_rows=kd),
        grid=(mp // tm, nj, nk),
        in_specs=[pl.BlockSpec((tm, TK_DOWN), lambda i, j, k: (i, k)),
                  pl.BlockSpec((ms, TK_DOWN), lambda i, j, k: (0, jnp.where(i == 0, k, nk - 1))),
                  pl.BlockSpec((None, TK_DOWN, TN_DOWN), lambda i, j, k: (l, k, j))],
        out_specs=[pl.BlockSpec((tm, TN_DOWN), lambda i, j, k: (i, j)),
                   pl.BlockSpec((ms, TN_DOWN), lambda i, j, k: (0, jnp.where(i == 0, j, nj - 1)))],
        out_shape=[jax.ShapeDtypeStruct((mp, n), F32), jax.ShapeDtypeStruct((ms, n), F32)],
        scratch_shapes=[pltpu.VMEM((TK_DOWN, TN_DOWN), BF16)],
        compiler_params=_params(("arbitrary", "arbitrary", "arbitrary")),
        name="ffn_down",
    )(gp, gs, w)


def _state_in_spec(grp, k1, idx):
    if grp.tile.phased:
        return pl.BlockSpec((None, 1, k1, CT), lambda *g: (grp.ls, idx(*g)[0], 0, idx(*g)[1]))
    return pl.BlockSpec((None, k1, grp.tile.h, CT), lambda *g: (grp.ls, 0, 0, idx(*g)[1]))


def _state_out(grp, k1, width, idx):
    if grp.tile.phased:
        return (pl.BlockSpec((1, k1, CT), lambda *g: (idx(*g)[0], 0, idx(*g)[1])),
                jax.ShapeDtypeStruct((grp.nseq, k1, width), F32))
    return (pl.BlockSpec((k1, grp.tile.h, CT), lambda *g: (0, 0, idx(*g)[1])),
            jax.ShapeDtypeStruct((k1, grp.nseq, width), F32))


def _prev_slabs(tile, x, st_ref, k1):
    if not tile.phased:
        return st_ref[...].reshape(k1 * tile.h, x.shape[-1])
    sub = lax.broadcasted_iota(jnp.int32, (tile.h, x.shape[-1]), 0)
    out = []
    for j in range(k1):
        slab = x[(tile.q - k1 + j) * tile.h:(tile.q - k1 + j + 1) * tile.h]
        out.append(jnp.where(sub == 0, st_ref[0, j:j + 1, :], pltpu.roll(slab, 1, axis=0)))
    return jnp.concatenate(out, axis=0)


def _store_new_state(tile, ns_ref, tail, k1):
    tail3 = tail.reshape(k1, tile.h, tail.shape[-1])
    if tile.phased:
        ns_ref[0] = tail3[:, tile.h - 1, :]
    else:
        ns_ref[...] = tail3


def _conv_value(tile, x, st_ref, ns_ref, w_ref, b_ref, k_taps):
    k1 = k_taps - 1
    xp = jnp.concatenate([_prev_slabs(tile, x, st_ref, k1), x], axis=0)
    _store_new_state(tile, ns_ref, xp[tile.rows:], k1)
    acc = b_ref[...] + w_ref[0:1, :] * xp[0:tile.rows]
    for k in range(1, k_taps):
        acc = acc + w_ref[k:k + 1, :] * xp[k * tile.h:k * tile.h + tile.rows]
    return acc


def _glu_to_scratch(tile, h_ref, w_scr, st_ref, ns_ref, xp_ref):
    k1 = K_A - 1
    head = k1 * tile.h
    r = _dot(h_ref[...], w_scr[...])
    ua = r[:, :CT] * _sigmoid(r[:, CT:])
    xp_ref[0:head, :] = _prev_slabs(tile, ua, st_ref, k1)
    xp_ref[head:, :] = ua
    _store_new_state(tile, ns_ref, xp_ref[tile.rows:, :], k1)


def _proj_a_body(hp_ref, hs_ref, wv_ref, wg_ref, stp_ref, sts_ref, wdw_prev_ref, bdw_prev_ref,
                 wdw_ref, bdw_ref, ucp_ref, nsp_ref, ucs_ref, nss_ref,
                 w_scr, xa_ref, xb_ref, xs_ref, wb_ref, *, tile_p, tile_s):
    s = pl.program_id(0)
    rows = min(CONV_ROWS, tile_p.rows)
    groups = rows // SUBLANES

    @pl.when(s == 0)
    def _():
        xb_ref[...] = jnp.zeros(xb_ref.shape, xb_ref.dtype)

    def step(xp_cur, xp_prev):
        for k in range(K_A):
            wb_ref[k] = jnp.broadcast_to(wdw_prev_ref[k:k + 1, :], (SUBLANES, CT))
        bias = jnp.broadcast_to(bdw_prev_ref[...], (groups, SUBLANES, CT))
        for r0 in range(0, tile_p.rows, rows):
            acc = bias
            for k in range(K_A):
                x = xp_prev[r0 + k * tile_p.h:r0 + k * tile_p.h + rows, :]
                acc = acc + wb_ref[k][None] * x.reshape(groups, SUBLANES, CT)
            ucp_ref[r0:r0 + rows, :] = acc.reshape(rows, CT)

        _cast_pair(w_scr, wv_ref, wg_ref)
        _glu_to_scratch(tile_p, hp_ref, w_scr, stp_ref, nsp_ref, xp_cur)

    @pl.when(s % 2 == 0)
    def _():
        step(xa_ref, xb_ref)

    @pl.when(s % 2 == 1)
    def _():
        step(xb_ref, xa_ref)

    @pl.when(s < NJ_BR)
    def _():
        _glu_to_scratch(tile_s, hs_ref, w_scr, sts_ref, nss_ref, xs_ref)
        srows = min(CONV_ROWS, tile_s.rows)
        bias = jnp.broadcast_to(bdw_ref[...], (srows, CT))

        def chunk(c, carry):
            r0 = pl.multiple_of(c * srows, srows)
            acc = bias
            for k in range(K_A):
                acc = acc + wdw_ref[k:k + 1, :] * xs_ref[pl.ds(pl.multiple_of(r0 + k * tile_s.h, SUBLANES), srows), :]
            ucs_ref[pl.ds(r0, srows), :] = acc
            return carry

        lax.fori_loop(0, tile_s.rows // srows, chunk, 0)


def _skewed(ni, nj):
    last = ni * nj - 1

    def cur(s):
        s = jnp.minimum(s, last)
        return s // nj, s % nj

    def prev(s):
        s = jnp.maximum(s - 1, 0)
        return s // nj, s % nj

    return cur, prev


def _proj_a(hp, hs, w_in, l, gp, gs, state_p, state_s, wdw, bdw):
    mp, d = hp.shape
    ms = hs.shape[0]
    tm = gp.tile.rows
    nj = NJ_BR
    ni = mp // tm
    k1 = K_A - 1
    cur, prev = _skewed(ni, nj)
    first = lambda s: (0, jnp.minimum(s, nj - 1))
    nsp_spec, nsp_shape = _state_out(gp, k1, W_BR, cur)
    nss_spec, nss_shape = _state_out(gs, k1, W_BR, first)
    return pl.pallas_call(
        functools.partial(_proj_a_body, tile_p=gp.tile, tile_s=gs.tile),
        grid=(ni * nj + 1,),
        in_specs=[_once((tm, d), lambda s: (cur(s)[0], 0)),
                  _once((ms, d), lambda s: (0, 0)),
                  pl.BlockSpec((None, d, CT), lambda s: (l, 0, cur(s)[1])),
                  pl.BlockSpec((None, d, CT), lambda s: (l, 0, nj + cur(s)[1])),
                  _state_in_spec(gp, k1, cur),
                  _state_in_spec(gs, k1, first),
                  pl.BlockSpec((None, K_A, CT), lambda s: (l, 0, prev(s)[1])),
                  pl.BlockSpec((None, 1, CT), lambda s: (l, 0, prev(s)[1])),
                  pl.BlockSpec((None, K_A, CT), lambda s: (l, 0, first(s)[1])),
                  pl.BlockSpec((None, 1, CT), lambda s: (l, 0, first(s)[1]))],
        out_specs=[pl.BlockSpec((tm, CT), lambda s: prev(s)), nsp_spec,
                   pl.BlockSpec((ms, CT), lambda s: first(s)), nss_spec],
        out_shape=[jax.ShapeDtypeStruct((mp, W_BR), F32), nsp_shape,
                   jax.ShapeDtypeStruct((ms, W_BR), F32), nss_shape],
        scratch_shapes=[pltpu.VMEM((d, 2 * CT), BF16),
                        pltpu.VMEM((k1 * gp.tile.h + tm, CT), F32),
                        pltpu.VMEM((k1 * gp.tile.h + tm, CT), F32),
                        pltpu.VMEM((k1 * gs.tile.h + ms, CT), F32),
                        pltpu.VMEM((K_A, SUBLANES, CT), F32)],
        compiler_params=_params(("arbitrary",)),
        name="proj_a",
    )(hp, hs, w_in, w_in, state_p, state_s, wdw, bdw, wdw, bdw)


def _lru_scan(tile, a_ref, u_ref, h0):
    hq = tile.h
    if not tile.phased:
        hcur = h0
        for q in range(tile.q):
            hcur = a_ref[q * hq:(q + 1) * hq, :] * hcur + u_ref[q * hq:(q + 1) * hq, :]
            u_ref[q * hq:(q + 1) * hq, :] = hcur
        return hcur

    def slab(q, carry):
        acum, z = carry
        r0 = pl.multiple_of(q * hq, hq)
        a = a_ref[pl.ds(r0, hq), :]
        acum = a * acum
        z = a * z + u_ref[pl.ds(r0, hq), :]
        a_ref[pl.ds(r0, hq), :] = acum
        u_ref[pl.ds(r0, hq), :] = z
        return acum, z

    ones = jnp.ones((hq, a_ref.shape[-1]), F32)
    a_end, z_end = lax.fori_loop(0, tile.q, slab, (ones, jnp.zeros_like(ones)), unroll=8)
    starts = []
    hcur = h0
    for p in range(hq):
        starts.append(hcur)
        hcur = a_end[p:p + 1, :] * hcur + z_end[p:p + 1, :]
    start = jnp.concatenate(starts, axis=0)
    c = a_ref.shape[-1]
    hs = a_ref[...].reshape(tile.q, hq, c) * start[None] + u_ref[...].reshape(tile.q, hq, c)
    u_ref[...] = hs.reshape(tile.rows, c)
    return hcur


def _proj_b_body(hp_ref, hs_ref, wx_ref, wg_ref, stp_ref, sts_ref, h0p_ref, h0s_ref, wdw_ref, bdw_ref,
                 wr_ref, br_ref, wi_ref, bi_ref, lam_ref,
                 ybp_ref, nsp_ref, nhp_ref, ybs_ref, nss_ref, nhs_ref,
                 w_scr, ap_ref, up_ref, as_ref, us_ref, *, tile_p, tile_s):
    def one(tile, h_ref, st_ref, h0_ref, yb_ref, ns_ref, nh_ref, a_ref, u_ref):
        r = _dot(h_ref[...], w_scr[...])
        bg = r[:, CT:]
        xb = _conv_value(tile, r[:, :CT], st_ref, ns_ref, wdw_ref, bdw_ref, K_B)

        xb16 = xb.astype(BF16)
        heads = range(CT // HEAD)
        r_lin = jnp.concatenate(
            [_dot(xb16[:, hd * HEAD:(hd + 1) * HEAD], wr_ref[hd].astype(BF16)) for hd in heads], axis=1)
        i_lin = jnp.concatenate(
            [_dot(xb16[:, hd * HEAD:(hd + 1) * HEAD], wi_ref[hd].astype(BF16)) for hd in heads], axis=1)
        rg = _sigmoid(r_lin + br_ref[...])
        ig = _sigmoid(i_lin + bi_ref[...])
        neg_lam = -lam_ref[...]
        softplus = jnp.maximum(neg_lam, 0.0) + jnp.log1p(jnp.exp(-jnp.abs(neg_lam)))
        log_a = (-LRU_C) * rg * softplus
        a_ref[...] = jnp.exp(log_a)
        th = jnp.tanh(log_a)
        u_ref[...] = jnp.sqrt(-2.0 * th / (1.0 - th)) * (ig * xb)

        h_last = _lru_scan(tile, a_ref, u_ref, h0_ref[...].reshape(-1, CT))
        nh_ref[...] = h_last.reshape(nh_ref.shape)
        yb_ref[...] = (_gelu(bg) * u_ref[...]).astype(yb_ref.dtype)

    _cast_pair(w_scr, wx_ref, wg_ref)
    one(tile_p, hp_ref, stp_ref, h0p_ref, ybp_ref, nsp_ref, nhp_ref, ap_ref, up_ref)

    @pl.when(pl.program_id(0) == 0)
    def _():
        one(tile_s, hs_ref, sts_ref, h0s_ref, ybs_ref, nss_ref, nhs_ref, as_ref, us_ref)


def _h0_specs(grp, idx):
    if grp.tile.phased:
        return (pl.BlockSpec((None, 1, 1, CT), lambda *g: (grp.ls, idx(*g)[0], 0, idx(*g)[1])),
                pl.BlockSpec((1, 1, CT), lambda *g: (idx(*g)[0], 0, idx(*g)[1])),
                jax.ShapeDtypeStruct((grp.nseq, 1, W_BR), F32))
    return (pl.BlockSpec((None, grp.tile.h, CT), lambda *g: (grp.ls, 0, idx(*g)[1])),
            pl.BlockSpec((grp.tile.h, CT), lambda *g: (0, idx(*g)[1])),
            jax.ShapeDtypeStruct((grp.nseq, W_BR), F32))


def _proj_b(hp, hs, w_in, l, gp, gs, state_p, state_s, h0p, h0s, wdw, bdw, wr, br, wi, bi, lam):
    mp, d = hp.shape
    ms = hs.shape[0]
    tm = gp.tile.rows
    nj = NJ_BR
    hpt = CT // HEAD
    k1 = K_B - 1
    sj = _first_tile_only(nj)
    idx_p = lambda i, j: (i, j)
    idx_s = lambda i, j: (0, sj(i, j))
    vec = pl.BlockSpec((None, 1, CT), lambda i, j: (l, 0, j))
    nsp_spec, nsp_shape = _state_out(gp, k1, W_BR, idx_p)
    nss_spec, nss_shape = _state_out(gs, k1, W_BR, idx_s)
    h0p_spec, nhp_spec, nhp_shape = _h0_specs(gp, idx_p)
    h0s_spec, nhs_spec, nhs_shape = _h0_specs(gs, idx_s)
    return pl.pallas_call(
        functools.partial(_proj_b_body, tile_p=gp.tile, tile_s=gs.tile),
        grid=(mp // tm, nj),
        in_specs=[_once((tm, d), lambda i, j: (i, 0)),
                  _once((ms, d), lambda i, j: (0, 0)),
                  pl.BlockSpec((None, d, CT), lambda i, j: (l, 0, 2 * nj + j)),
                  pl.BlockSpec((None, d, CT), lambda i, j: (l, 0, 3 * nj + j)),
                  _state_in_spec(gp, k1, idx_p), _state_in_spec(gs, k1, idx_s),
                  h0p_spec, h0s_spec,
                  pl.BlockSpec((None, K_B, CT), lambda i, j: (l, 0, j)),
                  vec,
                  pl.BlockSpec((None, hpt, HEAD, HEAD), lambda i, j: (l, j, 0, 0)),
                  vec,
                  pl.BlockSpec((None, hpt, HEAD, HEAD), lambda i, j: (l, j, 0, 0)),
                  vec,
                  vec],
        out_specs=[pl.BlockSpec((tm, CT), lambda i, j: (i, j)), nsp_spec, nhp_spec,
                   pl.BlockSpec((ms, CT), lambda i, j: (0, sj(i, j))), nss_spec, nhs_spec],
        out_shape=[jax.ShapeDtypeStruct((mp, W_BR), BF16), nsp_shape, nhp_shape,
                   jax.ShapeDtypeStruct((ms, W_BR), BF16), nss_shape, nhs_shape],
        scratch_shapes=[pltpu.VMEM((d, 2 * CT), BF16),
                        pltpu.VMEM((tm, CT), F32), pltpu.VMEM((tm, CT), F32),
                        pltpu.VMEM((ms, CT), F32), pltpu.VMEM((ms, CT), F32)],
        compiler_params=_params(("arbitrary", "arbitrary")),
        name="proj_b",
    )(hp, hs, w_in, w_in, state_p, state_s, h0p, h0s, wdw, bdw, wr, br, wi, bi, lam)


def _merge_body(ucp_ref, ybp_ref, gap_ref, gbp_ref, ucs_ref, ybs_ref, gas_ref, gbs_ref,
                wa_ref, wb_ref, lng_ref, lnb_ref, op_ref, os_ref,
                yap_ref, yas_ref, wa_scr, wb_scr):
    i = pl.program_id(0)
    j = pl.program_id(1)

    def norm_act(uc_ref, ya_ref):
        for r0 in range(0, uc_ref.shape[0], TM_ROWWISE):
            x = uc_ref[r0:r0 + TM_ROWWISE, :]
            mu = jnp.mean(x, axis=-1, keepdims=True)
            xc = x - mu
            var = jnp.mean(xc * xc, axis=-1, keepdims=True)
            y = xc * lax.rsqrt(var + EPS) * lng_ref[...] + lnb_ref[...]
            ya_ref[r0:r0 + TM_ROWWISE, :] = (y * _sigmoid(y)).astype(ya_ref.dtype)

    def one(ya_ref, yb_ref, ga_ref, gb_ref, o_ref):
        out_a = _dot(ya_ref[...], wa_scr[...])
        out_b = _dot(yb_ref[...], wb_scr[...])
        mixed = ga_ref[...].astype(F32) * out_a + gb_ref[...].astype(F32) * out_b
        o_ref[...] = mixed.astype(o_ref.dtype)

    @pl.when(j == 0)
    def _():
        norm_act(ucp_ref, yap_ref)

    @pl.when(jnp.logical_and(i == 0, j == 0))
    def _():
        norm_act(ucs_ref, yas_ref)

    wa_scr[...] = wa_ref[...].astype(BF16)
    wb_scr[...] = wb_ref[...].astype(BF16)
    one(yap_ref, ybp_ref, gap_ref, gbp_ref, op_ref)

    @pl.when(i == 0)
    def _():
        one(yas_ref, ybs_ref, gas_ref, gbs_ref, os_ref)


def _merge(ucp, ybp, gates_p, ucs, ybs, gates_s, wa, wb, ln_g, ln_b, l):
    mp = ucp.shape[0]
    ms = ucs.shape[0]
    tm = TM_MERGE
    nj = D_MODEL // TN
    sj = _first_tile_only(nj)
    return pl.pallas_call(
        _merge_body,
        grid=(mp // tm, nj),
        in_specs=[_once((tm, W_BR), lambda i, j: (i, 0)),
                  _once((tm, W_BR), lambda i, j: (i, 0)),
                  pl.BlockSpec((tm, TN), lambda i, j: (i, j)),
                  pl.BlockSpec((tm, TN), lambda i, j: (i, nj + j)),
                  _once((ms, W_BR), lambda i, j: (0, 0)),
                  _once((ms, W_BR), lambda i, j: (0, 0)),
                  pl.BlockSpec((ms, TN), lambda i, j: (0, sj(i, j))),
                  pl.BlockSpec((ms, TN), lambda i, j: (0, nj + sj(i, j))),
                  pl.BlockSpec((None, W_BR, TN), lambda i, j: (l, 0, j)),
                  pl.BlockSpec((None, W_BR, TN), lambda i, j: (l, 0, j)),
                  pl.BlockSpec((None, 1, W_BR), lambda i, j: (l, 0, 0)),
                  pl.BlockSpec((None, 1, W_BR), lambda i, j: (l, 0, 0))],
        out_specs=[pl.BlockSpec((tm, TN), lambda i, j: (i, j)),
                   pl.BlockSpec((ms, TN), lambda i, j: (0, sj(i, j)))],
        out_shape=[jax.ShapeDtypeStruct((mp, D_MODEL), BF16),
                   jax.ShapeDtypeStruct((ms, D_MODEL), BF16)],
        scratch_shapes=[pltpu.VMEM((tm, W_BR), BF16),
                        pltpu.VMEM((ms, W_BR), BF16),
                        pltpu.VMEM((W_BR, TN), BF16),
                        pltpu.VMEM((W_BR, TN), BF16)],
        compiler_params=_params(("arbitrary", "arbitrary")),
        name="merge",
    )(ucp, ybp, gates_p, gates_p, ucs, ybs, gates_s, gates_s, wa, wb, ln_g, ln_b)


def _ffn_up_body(hp_ref, hs_ref, w1_ref, w2_ref, st1p_ref, st2p_ref, st1s_ref, st2s_ref,
                 wd1_ref, wd2_ref, bd1_ref, bd2_ref,
                 gp_ref, ns1p_ref, ns2p_ref, gs_ref, ns1s_ref, ns2s_ref, w_scr, *, tile_p, tile_s):
    def one(tile, h_ref, st1_ref, st2_ref, g_ref, ns1_ref, ns2_ref):
        r = _dot(h_ref[...], w_scr[...])
        c1 = _conv_value(tile, r[:, :CT], st1_ref, ns1_ref, wd1_ref, bd1_ref, K_F)
        c2 = _conv_value(tile, r[:, CT:], st2_ref, ns2_ref, wd2_ref, bd2_ref, K_F)
        g_ref[...] = (_gelu(c1) * c2).astype(g_ref.dtype)

    _cast_pair(w_scr, w1_ref, w2_ref)
    one(tile_p, hp_ref, st1p_ref, st2p_ref, gp_ref, ns1p_ref, ns2p_ref)

    @pl.when(pl.program_id(0) == 0)
    def _():
        one(tile_s, hs_ref, st1s_ref, st2s_ref, gs_ref, ns1s_ref, ns2s_ref)


def _ffn_up(hp, hs, w_up, l, gp, gs, state_p, state_s, wdw, bdw):
    mp, d = hp.shape
    ms = hs.shape[0]
    tm = gp.tile.rows
    k1 = K_F - 1
    nj = NJ_FF
    sj = _first_tile_only(nj)
    lo_p = lambda i, j: (i, j)
    hi_p = lambda i, j: (i, nj + j)
    lo_s = lambda i, j: (0, sj(i, j))
    hi_s = lambda i, j: (0, nj + sj(i, j))
    nsp_spec, nsp_shape = _state_out(gp, k1, D_FF, lo_p)
    nss_spec, nss_shape = _state_out(gs, k1, D_FF, lo_s)
    return pl.pallas_call(
        functools.partial(_ffn_up_body, tile_p=gp.tile, tile_s=gs.tile),
        grid=(mp // tm, nj),
        in_specs=[_once((tm, d), lambda i, j: (i, 0)),
                  _once((ms, d), lambda i, j: (0, 0)),
                  pl.BlockSpec((None, d, CT), lambda i, j: (l, 0, j)),
                  pl.BlockSpec((None, d, CT), lambda i, j: (l, 0, nj + j)),
                  _state_in_spec(gp, k1, lo_p), _state_in_spec(gp, k1, hi_p),
                  _state_in_spec(gs, k1, lo_s), _state_in_spec(gs, k1, hi_s),
                  pl.BlockSpec((None, K_F, CT), lambda i, j: (l, 0, j)),
                  pl.BlockSpec((None, K_F, CT), lambda i, j: (l, 0, nj + j)),
                  pl.BlockSpec((None, 1, CT), lambda i, j: (l, 0, j)),
                  pl.BlockSpec((None, 1, CT), lambda i, j: (l, 0, nj + j))],
        out_specs=[pl.BlockSpec((tm, CT), lambda i, j: (i, j)), nsp_spec, nsp_spec,
                   pl.BlockSpec((ms, CT), lambda i, j: (0, sj(i, j))), nss_spec, nss_spec],
        out_shape=[jax.ShapeDtypeStruct((mp, D_FF), BF16), nsp_shape, nsp_shape,
                   jax.ShapeDtypeStruct((ms, D_FF), BF16), nss_shape, nss_shape],
        scratch_shapes=[pltpu.VMEM((d, 2 * CT), BF16)],
        compiler_params=_params(("arbitrary", "arbitrary")),
        name="ffn_up",
    )(hp, hs, w_up, w_up, state_p, state_p, state_s, state_s, wdw, wdw, bdw, bdw)


def _layer(xp, xs, states_p, states_s, gp, gs, w, l):
    tm = gp.tile.rows
    hp = _rms_cast(xp, w["g_pre_mix"], l)
    hs = _rms_cast(xs, w["g_pre_mix"], l)
    ucp, nap, ucs, nas = _proj_a(hp, hs, w["w_in"], l, gp, gs, states_p[0], states_s[0],
                                 w["w_dw_a"], w["b_dw_a"])
    ybp, nbp, nhp, ybs, nbs, nhs = _proj_b(hp, hs, w["w_in"], l, gp, gs, states_p[1], states_s[1],
                                           states_p[2], states_s[2], w["w_dw_b"], w["b_dw_b"],
                                           w["w_rg_r"], w["b_rg_r"], w["w_rg_i"], w["b_rg_i"],
                                           w["lru_lambda"])
    gtp, gts = _proj(hp, hs, w["w_in"], l, 4 * W_BR, 2 * D_MODEL, tm, _sigmoid, BF16, "proj_gates")
    mxp, mxs = _merge(ucp, ybp, gtp, ucs, ybs, gts, w["w_a_out"], w["w_b_out"],
                      w["ln_a_g"], w["ln_a_b"], l)
    pp, ps = _proj(mxp, mxs, w["w_o"], l, 0, D_MODEL, tm, None, F32, "proj_o")
    x1p, h2p = _resid_rms(xp, pp, w["g_post_mix"], w["g_pre_ffn"], l)
    x1s, h2s = _resid_rms(xs, ps, w["g_post_mix"], w["g_pre_ffn"], l)
    ggp, n1p, n2p, ggs, n1s, n2s = _ffn_up(h2p, h2s, w["w_up"], l, gp, gs, states_p[3], states_s[3],
                                           w["w_dw_f"], w["b_dw_f"])
    fp, fs = _ffn_down(ggp, ggs, w["w_down"], l, tm)
    yp = _resid_rms(x1p, fp, w["g_post_ffn"], None, l)
    ys = _resid_rms(x1s, fs, w["g_post_ffn"], None, l)
    return ((yp, nap, nbp, nhp, jnp.concatenate([n1p, n2p], axis=-1)),
            (ys, nas, nbs, nhs, jnp.concatenate([n1s, n2s], axis=-1)))


def kernel(x_prompt, x_sample, state_conv_a, state_conv_b, state_lru, state_ffn, g_pre_mix, w_in, w_dw_a, b_dw_a, ln_a_g, ln_a_b, w_a_out, w_dw_b, b_dw_b, w_rg_r, b_rg_r, w_rg_i, b_rg_i, lru_lambda, w_b_out, w_o, g_post_mix, g_pre_ffn, w_up, w_dw_f, b_dw_f, w_down, g_post_ffn):
    depth = w_in.shape[0]
    bp, tp, d = x_prompt.shape
    bs, ts, _ = x_sample.shape
    dt = x_prompt.dtype
    w = dict(w_in=w_in, w_dw_a=w_dw_a, w_a_out=w_a_out, w_dw_b=w_dw_b, w_rg_r=w_rg_r,
             w_rg_i=w_rg_i, w_b_out=w_b_out, w_o=w_o, w_up=w_up, w_dw_f=w_dw_f, w_down=w_down)
    vecs = dict(g_pre_mix=g_pre_mix, b_dw_a=b_dw_a, ln_a_g=ln_a_g, ln_a_b=ln_a_b, b_dw_b=b_dw_b,
                b_rg_r=b_rg_r, b_rg_i=b_rg_i, lru_lambda=lru_lambda, g_post_mix=g_post_mix,
                g_pre_ffn=g_pre_ffn, b_dw_f=b_dw_f, g_post_ffn=g_post_ffn)
    w.update({name: v[:, None, :] for name, v in vecs.items()})

    tile_p = Tile(q=tp // SUBLANES, h=SUBLANES, phased=True)
    yp = x_prompt.reshape(bp, SUBLANES, tile_p.q, d).swapaxes(1, 2).reshape(bp * tp, d)
    zeros_p = (jnp.zeros((1, bp, K_A - 1, W_BR), dt), jnp.zeros((1, bp, K_B - 1, W_BR), dt),
               jnp.zeros((1, bp, 1, W_BR), dt), jnp.zeros((1, bp, K_F - 1, 2 * D_FF), dt))
    tile_s = Tile(q=ts, h=bs, phased=False)
    ys = x_sample.swapaxes(0, 1).reshape(ts * bs, d)
    state_s = (state_conv_a.swapaxes(1, 2), state_conv_b.swapaxes(1, 2), state_lru,
               state_ffn.swapaxes(1, 2))

    outs_p = [[], [], [], []]
    outs_s = [[], [], [], []]
    for l in range(depth):
        (yp, *new_p), (ys, *new_s) = _layer(yp, ys, zeros_p, state_s, Group(tile_p, bp, 0),
                                            Group(tile_s, bs, l), w, l)
        new_p[2] = new_p[2][:, 0, :]
        new_s = [new_s[0].swapaxes(0, 1), new_s[1].swapaxes(0, 1), new_s[2], new_s[3].swapaxes(0, 1)]
        for acc, new in ((outs_p, new_p), (outs_s, new_s)):
            for lst, v in zip(acc, new):
                lst.append(v)
    yp = yp.reshape(bp, tile_p.q, SUBLANES, d).swapaxes(1, 2).reshape(bp, tp, d)
    ys = ys.reshape(ts, bs, d).swapaxes(0, 1)
    return (yp, ys, *[jnp.stack(v) for v in outs_p], *[jnp.stack(v) for v in outs_s])
```

```python
import dataclasses
import functools
import math

import jax
import jax.numpy as jnp
from jax import lax
from jax.experimental import pallas as pl
from jax.experimental.pallas import tpu as pltpu

F32 = jnp.float32
BF16 = jnp.bfloat16

D_MODEL = 4096
W_BR = D_MODEL // 2
N_HEADS = 16
HEAD = W_BR // N_HEADS
K_A = 31
K_B = 4
K_F = 3
D_FF = 11008
LRU_C = 8.0
EPS = 1e-6

V7X_VMEM_LIMIT = 62 * 1024 * 1024
SUBLANES = 8
CT = 256
NJ_BR = W_BR // CT
NJ_FF = D_FF // CT
TK_DOWN = 1024
TN_DOWN = 1024
CONV_ROWS = 32
TN = 512
TM_MERGE = 1024
TM_ROWWISE = 256


@dataclasses.dataclass(frozen=True)
class Tile:
    q: int
    h: int
    phased: bool

    @property
    def rows(self):
        return self.q * self.h


@dataclasses.dataclass(frozen=True)
class Group:
    tile: Tile
    nseq: int
    ls: int


def _params(sem):
    return pltpu.CompilerParams(dimension_semantics=sem, vmem_limit_bytes=V7X_VMEM_LIMIT)


def _dot(a, b):
    return jnp.dot(a, b, preferred_element_type=F32)


def _sigmoid(x):
    return 1.0 / (1.0 + jnp.exp(-x))


def _gelu(x):
    c = math.sqrt(2.0 / math.pi)
    return x * (0.5 * (1.0 + jnp.tanh(c * (x + 0.044715 * (x * x * x)))))


def _once(shape, index_map):
    return pl.BlockSpec(shape, index_map, pipeline_mode=pl.Buffered(1))


def _cast_pair(w_scr, wa_ref, wb_ref):
    n = wa_ref.shape[-1]
    w_scr[:, :n] = wa_ref[...].astype(BF16)
    w_scr[:, n:] = wb_ref[...].astype(BF16)


def _zero_bits(v):
    u = lax.bitcast_convert_type(v, jnp.uint32)
    return lax.shift_right_logical(lax.shift_right_logical(u, jnp.uint32(16)), jnp.uint32(16))


def _cast_pair_after(w_scr, wa_ref, wb_ref, tokens):
    k, n = wa_ref.shape
    rb = k // len(tokens)
    for b, tok in enumerate(tokens):
        bits = lax.bitcast_convert_type(wa_ref[b * rb:(b + 1) * rb, :], jnp.uint32)
        bits = bits.reshape(rb // SUBLANES, SUBLANES, n) | _zero_bits(tok)[None]
        w_scr[b * rb:(b + 1) * rb, :n] = lax.bitcast_convert_type(bits.reshape(rb, n), F32).astype(BF16)
    w_scr[:, n:] = wb_ref[...].astype(BF16)


def _first_tile_only(n):
    return lambda i, j: jnp.where(i == 0, j, n - 1)


def _rms_cast_body(x_ref, g_ref, o_ref):
    x = x_ref[...]
    y = x * lax.rsqrt(jnp.mean(x * x, axis=-1, keepdims=True) + EPS)
    o_ref[...] = (y * g_ref[...]).astype(o_ref.dtype)


def _rms_cast(x, g, l):
    m, d = x.shape
    tm = TM_ROWWISE
    return pl.pallas_call(
        _rms_cast_body,
        grid=(m // tm,),
        in_specs=[pl.BlockSpec((tm, d), lambda i: (i, 0)),
                  pl.BlockSpec((None, 1, d), lambda i: (l, 0, 0))],
        out_specs=pl.BlockSpec((tm, d), lambda i: (i, 0)),
        out_shape=jax.ShapeDtypeStruct((m, d), BF16),
        compiler_params=_params(("parallel",)),
        name="rms_cast",
    )(x, g)


def _resid_rms_body(x_ref, p_ref, g_ref, o_ref):
    p = p_ref[...]
    y = p * lax.rsqrt(jnp.mean(p * p, axis=-1, keepdims=True) + EPS)
    o_ref[...] = x_ref[...] + y * g_ref[...]


def _resid_rms_next_body(x_ref, p_ref, g_ref, gn_ref, o_ref, h_ref):
    _resid_rms_body(x_ref, p_ref, g_ref, o_ref)
    x1 = o_ref[...]
    h = x1 * lax.rsqrt(jnp.mean(x1 * x1, axis=-1, keepdims=True) + EPS)
    h_ref[...] = (h * gn_ref[...]).astype(h_ref.dtype)


def _resid_rms(x, p, g, g_next, l):
    m, d = x.shape
    tm = TM_ROWWISE
    row = pl.BlockSpec((tm, d), lambda i: (i, 0))
    vec = pl.BlockSpec((None, 1, d), lambda i: (l, 0, 0))
    if g_next is None:
        return pl.pallas_call(
            _resid_rms_body, grid=(m // tm,), in_specs=[row, row, vec], out_specs=row,
            out_shape=jax.ShapeDtypeStruct((m, d), F32),
            compiler_params=_params(("parallel",)), name="resid_rms",
        )(x, p, g)
    return pl.pallas_call(
        _resid_rms_next_body, grid=(m // tm,), in_specs=[row, row, vec, vec], out_specs=[row, row],
        out_shape=[jax.ShapeDtypeStruct((m, d), F32), jax.ShapeDtypeStruct((m, d), BF16)],
        compiler_params=_params(("parallel",)), name="resid_rms_next",
    )(x, p, g, g_next)


def _proj_body(ap_ref, as_ref, w_ref, op_ref, os_ref, w_scr, *, act):
    def one(a_ref, o_ref):
        r = _dot(a_ref[...], w_scr[...])
        if act is not None:
            r = act(r)
        o_ref[...] = r.astype(o_ref.dtype)

    w_scr[...] = w_ref[...].astype(BF16)
    one(ap_ref, op_ref)

    @pl.when(pl.program_id(0) == 0)
    def _():
        one(as_ref, os_ref)


def _proj(ap, as_, w, l, col0, ncols, tm, act, out_dtype, name):
    mp, d = ap.shape
    ms = as_.shape[0]
    jb = col0 // TN
    nj = ncols // TN
    sj = _first_tile_only(nj)
    return pl.pallas_call(
        functools.partial(_proj_body, act=act),
        grid=(mp // tm, nj),
        in_specs=[_once((tm, d), lambda i, j: (i, 0)),
                  _once((ms, d), lambda i, j: (0, 0)),
                  pl.BlockSpec((None, d, TN), lambda i, j: (l, 0, jb + j))],
        out_specs=[pl.BlockSpec((tm, TN), lambda i, j: (i, j)),
                   pl.BlockSpec((ms, TN), lambda i, j: (0, sj(i, j)))],
        out_shape=[jax.ShapeDtypeStruct((mp, ncols), out_dtype),
                   jax.ShapeDtypeStruct((ms, ncols), out_dtype)],
        scratch_shapes=[pltpu.VMEM((d, TN), BF16)],
        compiler_params=_params(("arbitrary", "arbitrary")),
        name=name,
    )(ap, as_, w)


def _down_body(ap_ref, as_ref, w_ref, op_ref, os_ref, w_scr, *, k_rows):
    k = pl.program_id(2)
    limit = k_rows - k * TK_DOWN
    row = lax.broadcasted_iota(jnp.int32, w_ref.shape, 0)
    col = lax.broadcasted_iota(jnp.int32, (1, TK_DOWN), 1)

    def one(a_ref, o_ref):
        @pl.when(k == 0)
        def _():
            o_ref[...] = jnp.zeros(o_ref.shape, o_ref.dtype)

        a = jnp.where(col < limit, a_ref[...], jnp.zeros((), a_ref.dtype))
        o_ref[...] += _dot(a, w_scr[...])

    w_scr[...] = jnp.where(row < limit, w_ref[...], 0.0).astype(BF16)
    one(ap_ref, op_ref)

    @pl.when(pl.program_id(0) == 0)
    def _():
        one(as_ref, os_ref)


def _ffn_down(gp, gs, w, l, tm):
    mp = gp.shape[0]
    ms = gs.shape[0]
    kd, n = w.shape[1:]
    nk = pl.cdiv(kd, TK_DOWN)
    nj = n // TN_DOWN
    return pl.pallas_call(
        functools.partial(_down_body, k_rows=kd),
        grid=(mp // tm, nj, nk),
        in_specs=[pl.BlockSpec((tm, TK_DOWN), lambda i, j, k: (i, k)),
                  pl.BlockSpec((ms, TK_DOWN), lambda i, j, k: (0, jnp.where(i == 0, k, nk - 1))),
                  pl.BlockSpec((None, TK_DOWN, TN_DOWN), lambda i, j, k: (l, k, j))],
        out_specs=[pl.BlockSpec((tm, TN_DOWN), lambda i, j, k: (i, j)),
                   pl.BlockSpec((ms, TN_DOWN), lambda i, j, k: (0, jnp.where(i == 0, j, nj - 1)))],
        out_shape=[jax.ShapeDtypeStruct((mp, n), F32), jax.ShapeDtypeStruct((ms, n), F32)],
        scratch_shapes=[pltpu.VMEM((TK_DOWN, TN_DOWN), BF16)],
        compiler_params=_params(("arbitrary", "arbitrary", "arbitrary")),
        name="ffn_down",
    )(gp, gs, w)


def _state_in_spec(grp, k1, idx):
    if grp.tile.phased:
        return pl.BlockSpec((None, 1, k1, CT), lambda *g: (grp.ls, idx(*g)[0], 0, idx(*g)[1]))
    return pl.BlockSpec((None, k1, grp.tile.h, CT), lambda *g: (grp.ls, 0, 0, idx(*g)[1]))


def _state_out(grp, k1, width, idx):
    if grp.tile.phased:
        return (pl.BlockSpec((1, k1, CT), lambda *g: (idx(*g)[0], 0, idx(*g)[1])),
                jax.ShapeDtypeStruct((grp.nseq, k1, width), F32))
    return (pl.BlockSpec((k1, grp.tile.h, CT), lambda *g: (0, 0, idx(*g)[1])),
            jax.ShapeDtypeStruct((k1, grp.nseq, width), F32))


def _prev_slabs(tile, x, st_ref, k1):
    if not tile.phased:
        return st_ref[...].reshape(k1 * tile.h, x.shape[-1])
    sub = lax.broadcasted_iota(jnp.int32, (tile.h, x.shape[-1]), 0)
    out = []
    for j in range(k1):
        slab = x[(tile.q - k1 + j) * tile.h:(tile.q - k1 + j + 1) * tile.h]
        out.append(jnp.where(sub == 0, st_ref[0, j:j + 1, :], pltpu.roll(slab, 1, axis=0)))
    return jnp.concatenate(out, axis=0)


def _store_new_state(tile, ns_ref, tail, k1):
    tail3 = tail.reshape(k1, tile.h, tail.shape[-1])
    if tile.phased:
        ns_ref[0] = tail3[:, tile.h - 1, :]
    else:
        ns_ref[...] = tail3


def _conv_value(tile, x, st_ref, ns_ref, w_ref, b_ref, k_taps):
    k1 = k_taps - 1
    xp = jnp.concatenate([_prev_slabs(tile, x, st_ref, k1), x], axis=0)
    _store_new_state(tile, ns_ref, xp[tile.rows:], k1)
    acc = b_ref[...] + w_ref[0:1, :] * xp[0:tile.rows]
    for k in range(1, k_taps):
        acc = acc + w_ref[k:k + 1, :] * xp[k * tile.h:k * tile.h + tile.rows]
    return acc


def _glu_to_scratch(tile, h_ref, w_scr, st_ref, ns_ref, xp_ref):
    k1 = K_A - 1
    head = k1 * tile.h
    r = _dot(h_ref[...], w_scr[...])
    ua = r[:, :CT] * _sigmoid(r[:, CT:])
    xp_ref[0:head, :] = _prev_slabs(tile, ua, st_ref, k1)
    xp_ref[head:, :] = ua
    _store_new_state(tile, ns_ref, xp_ref[tile.rows:, :], k1)


def _proj_a_body(hp_ref, hs_ref, wv_ref, wg_ref, stp_ref, sts_ref, wdw_prev_ref, bdw_prev_ref,
                 wdw_ref, bdw_ref, ucp_ref, nsp_ref, ucs_ref, nss_ref,
                 w_scr, xa_ref, xb_ref, xs_ref, wb_ref, *, tile_p, tile_s):
    s = pl.program_id(0)
    rows = min(CONV_ROWS, tile_p.rows)
    groups = rows // SUBLANES

    @pl.when(s == 0)
    def _():
        xb_ref[...] = jnp.zeros(xb_ref.shape, xb_ref.dtype)

    def step(xp_cur, xp_prev):
        for k in range(K_A):
            wb_ref[k] = jnp.broadcast_to(wdw_prev_ref[k:k + 1, :], (SUBLANES, CT))
        bias = jnp.broadcast_to(bdw_prev_ref[...], (groups, SUBLANES, CT))
        tokens = []
        for r0 in range(0, tile_p.rows, rows):
            acc = bias
            for k in range(K_A):
                x = xp_prev[r0 + k * tile_p.h:r0 + k * tile_p.h + rows, :]
                acc = acc + wb_ref[k][None] * x.reshape(groups, SUBLANES, CT)
            ucp_ref[r0:r0 + rows, :] = acc.reshape(rows, CT)
            tokens.append(acc[0])

        _cast_pair_after(w_scr, wv_ref, wg_ref, tokens)
        _glu_to_scratch(tile_p, hp_ref, w_scr, stp_ref, nsp_ref, xp_cur)

    @pl.when(s % 2 == 0)
    def _():
        step(xa_ref, xb_ref)

    @pl.when(s % 2 == 1)
    def _():
        step(xb_ref, xa_ref)

    @pl.when(s < NJ_BR)
    def _():
        _glu_to_scratch(tile_s, hs_ref, w_scr, sts_ref, nss_ref, xs_ref)
        srows = min(CONV_ROWS, tile_s.rows)
        bias = jnp.broadcast_to(bdw_ref[...], (srows, CT))

        def chunk(c, carry):
            r0 = pl.multiple_of(c * srows, srows)
            acc = bias
            for k in range(K_A):
                acc = acc + wdw_ref[k:k + 1, :] * xs_ref[pl.ds(pl.multiple_of(r0 + k * tile_s.h, SUBLANES), srows), :]
            ucs_ref[pl.ds(r0, srows), :] = acc
            return carry

        lax.fori_loop(0, tile_s.rows // srows, chunk, 0)


def _skewed(ni, nj):
    last = ni * nj - 1

    def cur(s):
        s = jnp.minimum(s, last)
        return s // nj, s % nj

    def prev(s):
        s = jnp.maximum(s - 1, 0)
        return s // nj, s % nj

    return cur, prev


def _proj_a(hp, hs, w_in, l, gp, gs, state_p, state_s, wdw, bdw):
    mp, d = hp.shape
    ms = hs.shape[0]
    tm = gp.tile.rows
    nj = NJ_BR
    ni = mp // tm
    k1 = K_A - 1
    cur, prev = _skewed(ni, nj)
    first = lambda s: (0, jnp.minimum(s, nj - 1))
    nsp_spec, nsp_shape = _state_out(gp, k1, W_BR, cur)
    nss_spec, nss_shape = _state_out(gs, k1, W_BR, first)
    return pl.pallas_call(
        functools.partial(_proj_a_body, tile_p=gp.tile, tile_s=gs.tile),
        grid=(ni * nj + 1,),
        in_specs=[_once((tm, d), lambda s: (cur(s)[0], 0)),
                  _once((ms, d), lambda s: (0, 0)),
                  pl.BlockSpec((None, d, CT), lambda s: (l, 0, cur(s)[1])),
                  pl.BlockSpec((None, d, CT), lambda s: (l, 0, nj + cur(s)[1])),
                  _state_in_spec(gp, k1, cur),
                  _state_in_spec(gs, k1, first),
                  pl.BlockSpec((None, K_A, CT), lambda s: (l, 0, prev(s)[1])),
                  pl.BlockSpec((None, 1, CT), lambda s: (l, 0, prev(s)[1])),
                  pl.BlockSpec((None, K_A, CT), lambda s: (l, 0, first(s)[1])),
                  pl.BlockSpec((None, 1, CT), lambda s: (l, 0, first(s)[1]))],
        out_specs=[pl.BlockSpec((tm, CT), lambda s: prev(s)), nsp_spec,
                   pl.BlockSpec((ms, CT), lambda s: first(s)), nss_spec],
        out_shape=[jax.ShapeDtypeStruct((mp, W_BR), F32), nsp_shape,
                   jax.ShapeDtypeStruct((ms, W_BR), F32), nss_shape],
        scratch_shapes=[pltpu.VMEM((d, 2 * CT), BF16),
                        pltpu.VMEM((k1 * gp.tile.h + tm, CT), F32),
                        pltpu.VMEM((k1 * gp.tile.h + tm, CT), F32),
                        pltpu.VMEM((k1 * gs.tile.h + ms, CT), F32),
                        pltpu.VMEM((K_A, SUBLANES, CT), F32)],
        compiler_params=_params(("arbitrary",)),
        name="proj_a",
    )(hp, hs, w_in, w_in, state_p, state_s, wdw, bdw, wdw, bdw)


def _lru_scan(tile, a_ref, u_ref, h0):
    hq = tile.h
    if not tile.phased:
        hcur = h0
        for q in range(tile.q):
            hcur = a_ref[q * hq:(q + 1) * hq, :] * hcur + u_ref[q * hq:(q + 1) * hq, :]
            u_ref[q * hq:(q + 1) * hq, :] = hcur
        return hcur

    def slab(q, carry):
        acum, z = carry
        r0 = pl.multiple_of(q * hq, hq)
        a = a_ref[pl.ds(r0, hq), :]
        acum = a * acum
        z = a * z + u_ref[pl.ds(r0, hq), :]
        a_ref[pl.ds(r0, hq), :] = acum
        u_ref[pl.ds(r0, hq), :] = z
        return acum, z

    ones = jnp.ones((hq, a_ref.shape[-1]), F32)
    a_end, z_end = lax.fori_loop(0, tile.q, slab, (ones, jnp.zeros_like(ones)), unroll=8)
    starts = []
    hcur = h0
    for p in range(hq):
        starts.append(hcur)
        hcur = a_end[p:p + 1, :] * hcur + z_end[p:p + 1, :]
    start = jnp.concatenate(starts, axis=0)
    c = a_ref.shape[-1]
    hs = a_ref[...].reshape(tile.q, hq, c) * start[None] + u_ref[...].reshape(tile.q, hq, c)
    u_ref[...] = hs.reshape(tile.rows, c)
    return hcur


def _proj_b_body(hp_ref, hs_ref, wx_ref, wg_ref, stp_ref, sts_ref, h0p_ref, h0s_ref, wdw_ref, bdw_ref,
                 wr_ref, br_ref, wi_ref, bi_ref, lam_ref,
                 ybp_ref, nsp_ref, nhp_ref, ybs_ref, nss_ref, nhs_ref,
                 w_scr, ap_ref, up_ref, as_ref, us_ref, *, tile_p, tile_s):
    def one(tile, h_ref, st_ref, h0_ref, yb_ref, ns_ref, nh_ref, a_ref, u_ref):
        r = _dot(h_ref[...], w_scr[...])
        bg = r[:, CT:]
        xb = _conv_value(tile, r[:, :CT], st_ref, ns_ref, wdw_ref, bdw_ref, K_B)

        xb16 = xb.astype(BF16)
        heads = range(CT // HEAD)
        r_lin = jnp.concatenate(
            [_dot(xb16[:, hd * HEAD:(hd + 1) * HEAD], wr_ref[hd].astype(BF16)) for hd in heads], axis=1)
        i_lin = jnp.concatenate(
            [_dot(xb16[:, hd * HEAD:(hd + 1) * HEAD], wi_ref[hd].astype(BF16)) for hd in heads], axis=1)
        rg = _sigmoid(r_lin + br_ref[...])
        ig = _sigmoid(i_lin + bi_ref[...])
        neg_lam = -lam_ref[...]
        softplus = jnp.maximum(neg_lam, 0.0) + jnp.log1p(jnp.exp(-jnp.abs(neg_lam)))
        log_a = (-LRU_C) * rg * softplus
        a_ref[...] = jnp.exp(log_a)
        th = jnp.tanh(log_a)
        u_ref[...] = jnp.sqrt(-2.0 * th / (1.0 - th)) * (ig * xb)

        h_last = _lru_scan(tile, a_ref, u_ref, h0_ref[...].reshape(-1, CT))
        nh_ref[...] = h_last.reshape(nh_ref.shape)
        yb_ref[...] = (_gelu(bg) * u_ref[...]).astype(yb_ref.dtype)

    _cast_pair(w_scr, wx_ref, wg_ref)
    one(tile_p, hp_ref, stp_ref, h0p_ref, ybp_ref, nsp_ref, nhp_ref, ap_ref, up_ref)

    @pl.when(pl.program_id(0) == 0)
    def _():
        one(tile_s, hs_ref, sts_ref, h0s_ref, ybs_ref, nss_ref, nhs_ref, as_ref, us_ref)


def _h0_specs(grp, idx):
    if grp.tile.phased:
        return (pl.BlockSpec((None, 1, 1, CT), lambda *g: (grp.ls, idx(*g)[0], 0, idx(*g)[1])),
                pl.BlockSpec((1, 1, CT), lambda *g: (idx(*g)[0], 0, idx(*g)[1])),
                jax.ShapeDtypeStruct((grp.nseq, 1, W_BR), F32))
    return (pl.BlockSpec((None, grp.tile.h, CT), lambda *g: (grp.ls, 0, idx(*g)[1])),
            pl.BlockSpec((grp.tile.h, CT), lambda *g: (0, idx(*g)[1])),
            jax.ShapeDtypeStruct((grp.nseq, W_BR), F32))


def _proj_b(hp, hs, w_in, l, gp, gs, state_p, state_s, h0p, h0s, wdw, bdw, wr, br, wi, bi, lam):
    mp, d = hp.shape
    ms = hs.shape[0]
    tm = gp.tile.rows
    nj = NJ_BR
    hpt = CT // HEAD
    k1 = K_B - 1
    sj = _first_tile_only(nj)
    idx_p = lambda i, j: (i, j)
    idx_s = lambda i, j: (0, sj(i, j))
    vec = pl.BlockSpec((None, 1, CT), lambda i, j: (l, 0, j))
    nsp_spec, nsp_shape = _state_out(gp, k1, W_BR, idx_p)
    nss_spec, nss_shape = _state_out(gs, k1, W_BR, idx_s)
    h0p_spec, nhp_spec, nhp_shape = _h0_specs(gp, idx_p)
    h0s_spec, nhs_spec, nhs_shape = _h0_specs(gs, idx_s)
    return pl.pallas_call(
        functools.partial(_proj_b_body, tile_p=gp.tile, tile_s=gs.tile),
        grid=(mp // tm, nj),
        in_specs=[_once((tm, d), lambda i, j: (i, 0)),
                  _once((ms, d), lambda i, j: (0, 0)),
                  pl.BlockSpec((None, d, CT), lambda i, j: (l, 0, 2 * nj + j)),
                  pl.BlockSpec((None, d, CT), lambda i, j: (l, 0, 3 * nj + j)),
                  _state_in_spec(gp, k1, idx_p), _state_in_spec(gs, k1, idx_s),
                  h0p_spec, h0s_spec,
                  pl.BlockSpec((None, K_B, CT), lambda i, j: (l, 0, j)),
                  vec,
                  pl.BlockSpec((None, hpt, HEAD, HEAD), lambda i, j: (l, j, 0, 0)),
                  vec,
                  pl.BlockSpec((None, hpt, HEAD, HEAD), lambda i, j: (l, j, 0, 0)),
                  vec,
                  vec],
        out_specs=[pl.BlockSpec((tm, CT), lambda i, j: (i, j)), nsp_spec, nhp_spec,
                   pl.BlockSpec((ms, CT), lambda i, j: (0, sj(i, j))), nss_spec, nhs_spec],
        out_shape=[jax.ShapeDtypeStruct((mp, W_BR), BF16), nsp_shape, nhp_shape,
                   jax.ShapeDtypeStruct((ms, W_BR), BF16), nss_shape, nhs_shape],
        scratch_shapes=[pltpu.VMEM((d, 2 * CT), BF16),
                        pltpu.VMEM((tm, CT), F32), pltpu.VMEM((tm, CT), F32),
                        pltpu.VMEM((ms, CT), F32), pltpu.VMEM((ms, CT), F32)],
        compiler_params=_params(("arbitrary", "arbitrary")),
        name="proj_b",
    )(hp, hs, w_in, w_in, state_p, state_s, h0p, h0s, wdw, bdw, wr, br, wi, bi, lam)


def _merge_body(ucp_ref, ybp_ref, gap_ref, gbp_ref, ucs_ref, ybs_ref, gas_ref, gbs_ref,
                wa_ref, wb_ref, lng_ref, lnb_ref, op_ref, os_ref,
                yap_ref, yas_ref, wa_scr, wb_scr):
    i = pl.program_id(0)
    j = pl.program_id(1)

    def norm_act(uc_ref, ya_ref):
        for r0 in range(0, uc_ref.shape[0], TM_ROWWISE):
            x = uc_ref[r0:r0 + TM_ROWWISE, :]
            mu = jnp.mean(x, axis=-1, keepdims=True)
            xc = x - mu
            var = jnp.mean(xc * xc, axis=-1, keepdims=True)
            y = xc * lax.rsqrt(var + EPS) * lng_ref[...] + lnb_ref[...]
            ya_ref[r0:r0 + TM_ROWWISE, :] = (y * _sigmoid(y)).astype(ya_ref.dtype)

    def one(ya_ref, yb_ref, ga_ref, gb_ref, o_ref):
        out_a = _dot(ya_ref[...], wa_scr[...])
        out_b = _dot(yb_ref[...], wb_scr[...])
        mixed = ga_ref[...].astype(F32) * out_a + gb_ref[...].astype(F32) * out_b
        o_ref[...] = mixed.astype(o_ref.dtype)

    @pl.when(j == 0)
    def _():
        norm_act(ucp_ref, yap_ref)

    @pl.when(jnp.logical_and(i == 0, j == 0))
    def _():
        norm_act(ucs_ref, yas_ref)

    wa_scr[...] = wa_ref[...].astype(BF16)
    wb_scr[...] = wb_ref[...].astype(BF16)
    one(yap_ref, ybp_ref, gap_ref, gbp_ref, op_ref)

    @pl.when(i == 0)
    def _():
        one(yas_ref, ybs_ref, gas_ref, gbs_ref, os_ref)


def _merge(ucp, ybp, gates_p, ucs, ybs, gates_s, wa, wb, ln_g, ln_b, l):
    mp = ucp.shape[0]
    ms = ucs.shape[0]
    tm = TM_MERGE
    nj = D_MODEL // TN
    sj = _first_tile_only(nj)
    return pl.pallas_call(
        _merge_body,
        grid=(mp // tm, nj),
        in_specs=[_once((tm, W_BR), lambda i, j: (i, 0)),
                  _once((tm, W_BR), lambda i, j: (i, 0)),
                  pl.BlockSpec((tm, TN), lambda i, j: (i, j)),
                  pl.BlockSpec((tm, TN), lambda i, j: (i, nj + j)),
                  _once((ms, W_BR), lambda i, j: (0, 0)),
                  _once((ms, W_BR), lambda i, j: (0, 0)),
                  pl.BlockSpec((ms, TN), lambda i, j: (0, sj(i, j))),
                  pl.BlockSpec((ms, TN), lambda i, j: (0, nj + sj(i, j))),
                  pl.BlockSpec((None, W_BR, TN), lambda i, j: (l, 0, j)),
                  pl.BlockSpec((None, W_BR, TN), lambda i, j: (l, 0, j)),
                  pl.BlockSpec((None, 1, W_BR), lambda i, j: (l, 0, 0)),
                  pl.BlockSpec((None, 1, W_BR), lambda i, j: (l, 0, 0))],
        out_specs=[pl.BlockSpec((tm, TN), lambda i, j: (i, j)),
                   pl.BlockSpec((ms, TN), lambda i, j: (0, sj(i, j)))],
        out_shape=[jax.ShapeDtypeStruct((mp, D_MODEL), BF16),
                   jax.ShapeDtypeStruct((ms, D_MODEL), BF16)],
        scratch_shapes=[pltpu.VMEM((tm, W_BR), BF16),
                        pltpu.VMEM((ms, W_BR), BF16),
                        pltpu.VMEM((W_BR, TN), BF16),
                        pltpu.VMEM((W_BR, TN), BF16)],
        compiler_params=_params(("arbitrary", "arbitrary")),
        name="merge",
    )(ucp, ybp, gates_p, gates_p, ucs, ybs, gates_s, gates_s, wa, wb, ln_g, ln_b)


def _ffn_up_body(hp_ref, hs_ref, w1_ref, w2_ref, st1p_ref, st2p_ref, st1s_ref, st2s_ref,
                 wd1_ref, wd2_ref, bd1_ref, bd2_ref,
                 gp_ref, ns1p_ref, ns2p_ref, gs_ref, ns1s_ref, ns2s_ref, w_scr, *, tile_p, tile_s):
    def one(tile, h_ref, st1_ref, st2_ref, g_ref, ns1_ref, ns2_ref):
        r = _dot(h_ref[...], w_scr[...])
        c1 = _conv_value(tile, r[:, :CT], st1_ref, ns1_ref, wd1_ref, bd1_ref, K_F)
        c2 = _conv_value(tile, r[:, CT:], st2_ref, ns2_ref, wd2_ref, bd2_ref, K_F)
        g_ref[...] = (_gelu(c1) * c2).astype(g_ref.dtype)

    _cast_pair(w_scr, w1_ref, w2_ref)
    one(tile_p, hp_ref, st1p_ref, st2p_ref, gp_ref, ns1p_ref, ns2p_ref)

    @pl.when(pl.program_id(0) == 0)
    def _():
        one(tile_s, hs_ref, st1s_ref, st2s_ref, gs_ref, ns1s_ref, ns2s_ref)


def _ffn_up(hp, hs, w_up, l, gp, gs, state_p, state_s, wdw, bdw):
    mp, d = hp.shape
    ms = hs.shape[0]
    tm = gp.tile.rows
    k1 = K_F - 1
    nj = NJ_FF
    sj = _first_tile_only(nj)
    lo_p = lambda i, j: (i, j)
    hi_p = lambda i, j: (i, nj + j)
    lo_s = lambda i, j: (0, sj(i, j))
    hi_s = lambda i, j: (0, nj + sj(i, j))
    nsp_spec, nsp_shape = _state_out(gp, k1, D_FF, lo_p)
    nss_spec, nss_shape = _state_out(gs, k1, D_FF, lo_s)
    return pl.pallas_call(
        functools.partial(_ffn_up_body, tile_p=gp.tile, tile_s=gs.tile),
        grid=(mp // tm, nj),
        in_specs=[_once((tm, d), lambda i, j: (i, 0)),
                  _once((ms, d), lambda i, j: (0, 0)),
                  pl.BlockSpec((None, d, CT), lambda i, j: (l, 0, j)),
                  pl.BlockSpec((None, d, CT), lambda i, j: (l, 0, nj + j)),
                  _state_in_spec(gp, k1, lo_p), _state_in_spec(gp, k1, hi_p),
                  _state_in_spec(gs, k1, lo_s), _state_in_spec(gs, k1, hi_s),
                  pl.BlockSpec((None, K_F, CT), lambda i, j: (l, 0, j)),
                  pl.BlockSpec((None, K_F, CT), lambda i, j: (l, 0, nj + j)),
                  pl.BlockSpec((None, 1, CT), lambda i, j: (l, 0, j)),
                  pl.BlockSpec((None, 1, CT), lambda i, j: (l, 0, nj + j))],
        out_specs=[pl.BlockSpec((tm, CT), lambda i, j: (i, j)), nsp_spec, nsp_spec,
                   pl.BlockSpec((ms, CT), lambda i, j: (0, sj(i, j))), nss_spec, nss_spec],
        out_shape=[jax.ShapeDtypeStruct((mp, D_FF), BF16), nsp_shape, nsp_shape,
                   jax.ShapeDtypeStruct((ms, D_FF), BF16), nss_shape, nss_shape],
        scratch_shapes=[pltpu.VMEM((d, 2 * CT), BF16)],
        compiler_params=_params(("arbitrary", "arbitrary")),
        name="ffn_up",
    )(hp, hs, w_up, w_up, state_p, state_p, state_s, state_s, wdw, wdw, bdw, bdw)


def _layer(xp, xs, states_p, states_s, gp, gs, w, l):
    tm = gp.tile.rows
    hp = _rms_cast(xp, w["g_pre_mix"], l)
    hs = _rms_cast(xs, w["g_pre_mix"], l)
    ucp, nap, ucs, nas = _proj_a(hp, hs, w["w_in"], l, gp, gs, states_p[0], states_s[0],
                                 w["w_dw_a"], w["b_dw_a"])
    ybp, nbp, nhp, ybs, nbs, nhs = _proj_b(hp, hs, w["w_in"], l, gp, gs, states_p[1], states_s[1],
                                           states_p[2], states_s[2], w["w_dw_b"], w["b_dw_b"],
                                           w["w_rg_r"], w["b_rg_r"], w["w_rg_i"], w["b_rg_i"],
                                           w["lru_lambda"])
    gtp, gts = _proj(hp, hs, w["w_in"], l, 4 * W_BR, 2 * D_MODEL, tm, _sigmoid, BF16, "proj_gates")
    mxp, mxs = _merge(ucp, ybp, gtp, ucs, ybs, gts, w["w_a_out"], w["w_b_out"],
                      w["ln_a_g"], w["ln_a_b"], l)
    pp, ps = _proj(mxp, mxs, w["w_o"], l, 0, D_MODEL, tm, None, F32, "proj_o")
    x1p, h2p = _resid_rms(xp, pp, w["g_post_mix"], w["g_pre_ffn"], l)
    x1s, h2s = _resid_rms(xs, ps, w["g_post_mix"], w["g_pre_ffn"], l)
    ggp, n1p, n2p, ggs, n1s, n2s = _ffn_up(h2p, h2s, w["w_up"], l, gp, gs, states_p[3], states_s[3],
                                           w["w_dw_f"], w["b_dw_f"])
    fp, fs = _ffn_down(ggp, ggs, w["w_down"], l, tm)
    yp = _resid_rms(x1p, fp, w["g_post_ffn"], None, l)
    ys = _resid_rms(x1s, fs, w["g_post_ffn"], None, l)
    return ((yp, nap, nbp, nhp, jnp.concatenate([n1p, n2p], axis=-1)),
            (ys, nas, nbs, nhs, jnp.concatenate([n1s, n2s], axis=-1)))


def kernel(x_prompt, x_sample, state_conv_a, state_conv_b, state_lru, state_ffn, g_pre_mix, w_in, w_dw_a, b_dw_a, ln_a_g, ln_a_b, w_a_out, w_dw_b, b_dw_b, w_rg_r, b_rg_r, w_rg_i, b_rg_i, lru_lambda, w_b_out, w_o, g_post_mix, g_pre_ffn, w_up, w_dw_f, b_dw_f, w_down, g_post_ffn):
    depth = w_in.shape[0]
    bp, tp, d = x_prompt.shape
    bs, ts, _ = x_sample.shape
    dt = x_prompt.dtype
    w = dict(w_in=w_in, w_dw_a=w_dw_a, w_a_out=w_a_out, w_dw_b=w_dw_b, w_rg_r=w_rg_r,
             w_rg_i=w_rg_i, w_b_out=w_b_out, w_o=w_o, w_up=w_up, w_dw_f=w_dw_f, w_down=w_down)
    vecs = dict(g_pre_mix=g_pre_mix, b_dw_a=b_dw_a, ln_a_g=ln_a_g, ln_a_b=ln_a_b, b_dw_b=b_dw_b,
                b_rg_r=b_rg_r, b_rg_i=b_rg_i, lru_lambda=lru_lambda, g_post_mix=g_post_mix,
                g_pre_ffn=g_pre_ffn, b_dw_f=b_dw_f, g_post_ffn=g_post_ffn)
    w.update({name: v[:, None, :] for name, v in vecs.items()})

    tile_p = Tile(q=tp // SUBLANES, h=SUBLANES, phased=True)
    yp = x_prompt.reshape(bp, SUBLANES, tile_p.q, d).swapaxes(1, 2).reshape(bp * tp, d)
    zeros_p = (jnp.zeros((1, bp, K_A - 1, W_BR), dt), jnp.zeros((1, bp, K_B - 1, W_BR), dt),
               jnp.zeros((1, bp, 1, W_BR), dt), jnp.zeros((1, bp, K_F - 1, 2 * D_FF), dt))
    tile_s = Tile(q=ts, h=bs, phased=False)
    ys = x_sample.swapaxes(0, 1).reshape(ts * bs, d)
    state_s = (state_conv_a.swapaxes(1, 2), state_conv_b.swapaxes(1, 2), state_lru,
               state_ffn.swapaxes(1, 2))

    outs_p = [[], [], [], []]
    outs_s = [[], [], [], []]
    for l in range(depth):
        (yp, *new_p), (ys, *new_s) = _layer(yp, ys, zeros_p, state_s, Group(tile_p, bp, 0),
                                            Group(tile_s, bs, l), w, l)
        new_p[2] = new_p[2][:, 0, :]
        new_s = [new_s[0].swapaxes(0, 1), new_s[1].swapaxes(0, 1), new_s[2], new_s[3].swapaxes(0, 1)]
        for acc, new in ((outs_p, new_p), (outs_s, new_s)):
            for lst, v in zip(acc, new):
                lst.append(v)
    yp = yp.reshape(bp, tile_p.q, SUBLANES, d).swapaxes(1, 2).reshape(bp, tp, d)
    ys = ys.reshape(ts, bs, d).swapaxes(0, 1)
    return (yp, ys, *[jnp.stack(v) for v in outs_p], *[jnp.stack(v) for v in outs_s])
```

```python
import dataclasses
import functools
import math

import jax
import jax.numpy as jnp
from jax import lax
from jax.experimental import pallas as pl
from jax.experimental.pallas import tpu as pltpu

F32 = jnp.float32
BF16 = jnp.bfloat16

D_MODEL = 4096
W_BR = D_MODEL // 2
N_HEADS = 16
HEAD = W_BR // N_HEADS
K_A = 31
K_B = 4
K_F = 3
D_FF = 11008
LRU_C = 8.0
EPS = 1e-6

V7X_VMEM_LIMIT = 62 * 1024 * 1024
SUBLANES = 8
CT = 256
NJ_BR = W_BR // CT
NJ_FF = D_FF // CT
TK_DOWN = 1024
TN_DOWN = 1024
CONV_ROWS = 32
TN = 512
TM_MERGE = 1024
TM_ROWWISE = 256


@dataclasses.dataclass(frozen=True)
class Tile:
    q: int
    h: int
    phased: bool

    @property
    def rows(self):
        return self.q * self.h


@dataclasses.dataclass(frozen=True)
class Group:
    tile: Tile
    nseq: int
    ls: int


def _params(sem):
    return pltpu.CompilerParams(dimension_semantics=sem, vmem_limit_bytes=V7X_VMEM_LIMIT)


def _dot(a, b):
    return jnp.dot(a, b, preferred_element_type=F32)


def _sigmoid(x):
    return 1.0 / (1.0 + jnp.exp(-x))


def _gelu(x):
    c = math.sqrt(2.0 / math.pi)
    return x * (0.5 * (1.0 + jnp.tanh(c * (x + 0.044715 * (x * x * x)))))


def _once(shape, index_map):
    return pl.BlockSpec(shape, index_map, pipeline_mode=pl.Buffered(1))


def _cast_pair(w_scr, wa_ref, wb_ref):
    n = wa_ref.shape[-1]
    w_scr[:, :n] = wa_ref[...].astype(BF16)
    w_scr[:, n:] = wb_ref[...].astype(BF16)


def _zero_bits(v):
    u = lax.bitcast_convert_type(v, jnp.uint32)
    return lax.shift_right_logical(lax.shift_right_logical(u, jnp.uint32(16)), jnp.uint32(16))


def _cast_pair_after(w_scr, wa_ref, wb_ref, tokens):
    k, n = wa_ref.shape
    rb = k // len(tokens)
    for b, tok in enumerate(tokens):
        bits = lax.bitcast_convert_type(wa_ref[b * rb:(b + 1) * rb, :], jnp.uint32)
        bits = bits.reshape(rb // SUBLANES, SUBLANES, n) | _zero_bits(tok)[None]
        w_scr[b * rb:(b + 1) * rb, :n] = lax.bitcast_convert_type(bits.reshape(rb, n), F32).astype(BF16)
    w_scr[:, n:] = wb_ref[...].astype(BF16)


def _first_tile_only(n):
    return lambda i, j: jnp.where(i == 0, j, n - 1)


def _rms_cast_body(x_ref, g_ref, o_ref):
    x = x_ref[...]
    y = x * lax.rsqrt(jnp.mean(x * x, axis=-1, keepdims=True) + EPS)
    o_ref[...] = (y * g_ref[...]).astype(o_ref.dtype)


def _rms_cast(x, g, l):
    m, d = x.shape
    tm = TM_ROWWISE
    return pl.pallas_call(
        _rms_cast_body,
        grid=(m // tm,),
        in_specs=[pl.BlockSpec((tm, d), lambda i: (i, 0)),
                  pl.BlockSpec((None, 1, d), lambda i: (l, 0, 0))],
        out_specs=pl.BlockSpec((tm, d), lambda i: (i, 0)),
        out_shape=jax.ShapeDtypeStruct((m, d), BF16),
        compiler_params=_params(("parallel",)),
        name="rms_cast",
    )(x, g)


def _resid_rms_body(x_ref, p_ref, g_ref, o_ref):
    p = p_ref[...]
    y = p * lax.rsqrt(jnp.mean(p * p, axis=-1, keepdims=True) + EPS)
    o_ref[...] = x_ref[...] + y * g_ref[...]


def _resid_rms_next_body(x_ref, p_ref, g_ref, gn_ref, o_ref, h_ref):
    _resid_rms_body(x_ref, p_ref, g_ref, o_ref)
    x1 = o_ref[...]
    h = x1 * lax.rsqrt(jnp.mean(x1 * x1, axis=-1, keepdims=True) + EPS)
    h_ref[...] = (h * gn_ref[...]).astype(h_ref.dtype)


def _resid_rms(x, p, g, g_next, l):
    m, d = x.shape
    tm = TM_ROWWISE
    row = pl.BlockSpec((tm, d), lambda i: (i, 0))
    vec = pl.BlockSpec((None, 1, d), lambda i: (l, 0, 0))
    if g_next is None:
        return pl.pallas_call(
            _resid_rms_body, grid=(m // tm,), in_specs=[row, row, vec], out_specs=row,
            out_shape=jax.ShapeDtypeStruct((m, d), F32),
            compiler_params=_params(("parallel",)), name="resid_rms",
        )(x, p, g)
    return pl.pallas_call(
        _resid_rms_next_body, grid=(m // tm,), in_specs=[row, row, vec, vec], out_specs=[row, row],
        out_shape=[jax.ShapeDtypeStruct((m, d), F32), jax.ShapeDtypeStruct((m, d), BF16)],
        compiler_params=_params(("parallel",)), name="resid_rms_next",
    )(x, p, g, g_next)


def _proj_body(ap_ref, as_ref, w_ref, op_ref, os_ref, w_scr, *, act):
    def one(a_ref, o_ref):
        r = _dot(a_ref[...], w_scr[...])
        if act is not None:
            r = act(r)
        o_ref[...] = r.astype(o_ref.dtype)

    w_scr[...] = w_ref[...].astype(BF16)
    one(ap_ref, op_ref)

    @pl.when(pl.program_id(0) == 0)
    def _():
        one(as_ref, os_ref)


def _proj(ap, as_, w, l, col0, ncols, tm, act, out_dtype, name):
    mp, d = ap.shape
    ms = as_.shape[0]
    jb = col0 // TN
    nj = ncols // TN
    sj = _first_tile_only(nj)
    return pl.pallas_call(
        functools.partial(_proj_body, act=act),
        grid=(mp // tm, nj),
        in_specs=[_once((tm, d), lambda i, j: (i, 0)),
                  _once((ms, d), lambda i, j: (0, 0)),
                  pl.BlockSpec((None, d, TN), lambda i, j: (l, 0, jb + j))],
        out_specs=[pl.BlockSpec((tm, TN), lambda i, j: (i, j)),
                   pl.BlockSpec((ms, TN), lambda i, j: (0, sj(i, j)))],
        out_shape=[jax.ShapeDtypeStruct((mp, ncols), out_dtype),
                   jax.ShapeDtypeStruct((ms, ncols), out_dtype)],
        scratch_shapes=[pltpu.VMEM((d, TN), BF16)],
        compiler_params=_params(("arbitrary", "arbitrary")),
        name=name,
    )(ap, as_, w)


def _down_body(ap_ref, as_ref, w_ref, op_ref, os_ref, w_scr, *, k_rows):
    k = pl.program_id(2)
    limit = k_rows - k * TK_DOWN
    row = lax.broadcasted_iota(jnp.int32, w_ref.shape, 0)
    col = lax.broadcasted_iota(jnp.int32, (1, TK_DOWN), 1)

    def one(a_ref, o_ref):
        @pl.when(k == 0)
        def _():
            o_ref[...] = jnp.zeros(o_ref.shape, o_ref.dtype)

        a = jnp.where(col < limit, a_ref[...], jnp.zeros((), a_ref.dtype))
        o_ref[...] += _dot(a, w_scr[...])

    w_scr[...] = jnp.where(row < limit, w_ref[...], 0.0).astype(BF16)
    one(ap_ref, op_ref)

    @pl.when(pl.program_id(0) == 0)
    def _():
        one(as_ref, os_ref)


def _ffn_down(gp, gs, w, l, tm):
    mp = gp.shape[0]
    ms = gs.shape[0]
    kd, n = w.shape[1:]
    nk = pl.cdiv(kd, TK_DOWN)
    nj = n // TN_DOWN
    return pl.pallas_call(
        functools.partial(_down_body, k_rows=kd),
        grid=(mp // tm, nj, nk),
        in_specs=[pl.BlockSpec((tm, TK_DOWN), lambda i, j, k: (i, k)),
                  pl.BlockSpec((ms, TK_DOWN), lambda i, j, k: (0, jnp.where(i == 0, k, nk - 1))),
                  pl.BlockSpec((None, TK_DOWN, TN_DOWN), lambda i, j, k: (l, k, j))],
        out_specs=[pl.BlockSpec((tm, TN_DOWN), lambda i, j, k: (i, j)),
                   pl.BlockSpec((ms, TN_DOWN), lambda i, j, k: (0, jnp.where(i == 0, j, nj - 1)))],
        out_shape=[jax.ShapeDtypeStruct((mp, n), F32), jax.ShapeDtypeStruct((ms, n), F32)],
        scratch_shapes=[pltpu.VMEM((TK_DOWN, TN_DOWN), BF16)],
        compiler_params=_params(("arbitrary", "arbitrary", "arbitrary")),
        name="ffn_down",
    )(gp, gs, w)


def _state_in_spec(grp, k1, idx):
    if grp.tile.phased:
        return pl.BlockSpec((None, 1, k1, CT), lambda *g: (grp.ls, idx(*g)[0], 0, idx(*g)[1]))
    return pl.BlockSpec((None, k1, grp.tile.h, CT), lambda *g: (grp.ls, 0, 0, idx(*g)[1]))


def _state_out(grp, k1, width, idx):
    if grp.tile.phased:
        return (pl.BlockSpec((1, k1, CT), lambda *g: (idx(*g)[0], 0, idx(*g)[1])),
                jax.ShapeDtypeStruct((grp.nseq, k1, width), F32))
    return (pl.BlockSpec((k1, grp.tile.h, CT), lambda *g: (0, 0, idx(*g)[1])),
            jax.ShapeDtypeStruct((k1, grp.nseq, width), F32))


def _prev_slabs(tile, x, st_ref, k1):
    if not tile.phased:
        return st_ref[...].reshape(k1 * tile.h, x.shape[-1])
    sub = lax.broadcasted_iota(jnp.int32, (tile.h, x.shape[-1]), 0)
    out = []
    for j in range(k1):
        slab = x[(tile.q - k1 + j) * tile.h:(tile.q - k1 + j + 1) * tile.h]
        out.append(jnp.where(sub == 0, st_ref[0, j:j + 1, :], pltpu.roll(slab, 1, axis=0)))
    return jnp.concatenate(out, axis=0)


def _store_new_state(tile, ns_ref, tail, k1):
    tail3 = tail.reshape(k1, tile.h, tail.shape[-1])
    if tile.phased:
        ns_ref[0] = tail3[:, tile.h - 1, :]
    else:
        ns_ref[...] = tail3


def _conv_value(tile, x, st_ref, ns_ref, w_ref, b_ref, k_taps):
    k1 = k_taps - 1
    xp = jnp.concatenate([_prev_slabs(tile, x, st_ref, k1), x], axis=0)
    _store_new_state(tile, ns_ref, xp[tile.rows:], k1)
    acc = b_ref[...] + w_ref[0:1, :] * xp[0:tile.rows]
    for k in range(1, k_taps):
        acc = acc + w_ref[k:k + 1, :] * xp[k * tile.h:k * tile.h + tile.rows]
    return acc


def _glu_to_scratch(tile, h_ref, w_scr, st_ref, ns_ref, xp_ref):
    k1 = K_A - 1
    head = k1 * tile.h
    r = _dot(h_ref[...], w_scr[...])
    ua = r[:, :CT] * _sigmoid(r[:, CT:])
    xp_ref[0:head, :] = _prev_slabs(tile, ua, st_ref, k1)
    xp_ref[head:, :] = ua
    _store_new_state(tile, ns_ref, xp_ref[tile.rows:, :], k1)


def _proj_a_body(hp_ref, hs_ref, wv_ref, wg_ref, stp_ref, sts_ref, wdw_prev_ref, bdw_prev_ref,
                 wdw_ref, bdw_ref, ucp_ref, nsp_ref, ucs_ref, nss_ref,
                 w_scr, xa_ref, xb_ref, xs_ref, wb_ref, *, tile_p, tile_s):
    s = pl.program_id(0)
    rows = min(CONV_ROWS, tile_p.rows)
    groups = rows // SUBLANES

    @pl.when(s == 0)
    def _():
        xb_ref[...] = jnp.zeros(xb_ref.shape, xb_ref.dtype)

    def step(xp_cur, xp_prev):
        for k in range(K_A):
            wb_ref[k] = jnp.broadcast_to(wdw_prev_ref[k:k + 1, :], (SUBLANES, CT))
        bias = jnp.broadcast_to(bdw_prev_ref[...], (groups, SUBLANES, CT))
        tokens = []
        for r0 in range(0, tile_p.rows, rows):
            acc = bias
            for k in range(K_A):
                x = xp_prev[r0 + k * tile_p.h:r0 + k * tile_p.h + rows, :]
                acc = acc + wb_ref[k][None] * x.reshape(groups, SUBLANES, CT)
            ucp_ref[r0:r0 + rows, :] = acc.reshape(rows, CT)
            tokens.append(acc[0])

        _cast_pair_after(w_scr, wv_ref, wg_ref, tokens)
        _glu_to_scratch(tile_p, hp_ref, w_scr, stp_ref, nsp_ref, xp_cur)

    @pl.when(s % 2 == 0)
    def _():
        step(xa_ref, xb_ref)

    @pl.when(s % 2 == 1)
    def _():
        step(xb_ref, xa_ref)

    @pl.when(s < NJ_BR)
    def _():
        _glu_to_scratch(tile_s, hs_ref, w_scr, sts_ref, nss_ref, xs_ref)
        srows = min(CONV_ROWS, tile_s.rows)
        bias = jnp.broadcast_to(bdw_ref[...], (srows, CT))

        def chunk(c, carry):
            r0 = pl.multiple_of(c * srows, srows)
            acc = bias
            for k in range(K_A):
                acc = acc + wdw_ref[k:k + 1, :] * xs_ref[pl.ds(pl.multiple_of(r0 + k * tile_s.h, SUBLANES), srows), :]
            ucs_ref[pl.ds(r0, srows), :] = acc
            return carry

        lax.fori_loop(0, tile_s.rows // srows, chunk, 0)


def _skewed(ni, nj):
    last = ni * nj - 1

    def cur(s):
        s = jnp.minimum(s, last)
        return s // nj, s % nj

    def prev(s):
        s = jnp.maximum(s - 1, 0)
        return s // nj, s % nj

    return cur, prev


def _proj_a(hp, hs, w_in, l, gp, gs, state_p, state_s, wdw, bdw):
    mp, d = hp.shape
    ms = hs.shape[0]
    tm = gp.tile.rows
    nj = NJ_BR
    ni = mp // tm
    k1 = K_A - 1
    cur, prev = _skewed(ni, nj)
    first = lambda s: (0, jnp.minimum(s, nj - 1))
    nsp_spec, nsp_shape = _state_out(gp, k1, W_BR, cur)
    nss_spec, nss_shape = _state_out(gs, k1, W_BR, first)
    return pl.pallas_call(
        functools.partial(_proj_a_body, tile_p=gp.tile, tile_s=gs.tile),
        grid=(ni * nj + 1,),
        in_specs=[_once((tm, d), lambda s: (cur(s)[0], 0)),
                  _once((ms, d), lambda s: (0, 0)),
                  pl.BlockSpec((None, d, CT), lambda s: (l, 0, cur(s)[1])),
                  pl.BlockSpec((None, d, CT), lambda s: (l, 0, nj + cur(s)[1])),
                  _state_in_spec(gp, k1, cur),
                  _state_in_spec(gs, k1, first),
                  pl.BlockSpec((None, K_A, CT), lambda s: (l, 0, prev(s)[1])),
                  pl.BlockSpec((None, 1, CT), lambda s: (l, 0, prev(s)[1])),
                  pl.BlockSpec((None, K_A, CT), lambda s: (l, 0, first(s)[1])),
                  pl.BlockSpec((None, 1, CT), lambda s: (l, 0, first(s)[1]))],
        out_specs=[pl.BlockSpec((tm, CT), lambda s: prev(s)), nsp_spec,
                   pl.BlockSpec((ms, CT), lambda s: first(s)), nss_spec],
        out_shape=[jax.ShapeDtypeStruct((mp, W_BR), F32), nsp_shape,
                   jax.ShapeDtypeStruct((ms, W_BR), F32), nss_shape],
        scratch_shapes=[pltpu.VMEM((d, 2 * CT), BF16),
                        pltpu.VMEM((k1 * gp.tile.h + tm, CT), F32),
                        pltpu.VMEM((k1 * gp.tile.h + tm, CT), F32),
                        pltpu.VMEM((k1 * gs.tile.h + ms, CT), F32),
                        pltpu.VMEM((K_A, SUBLANES, CT), F32)],
        compiler_params=_params(("arbitrary",)),
        name="proj_a",
    )(hp, hs, w_in, w_in, state_p, state_s, wdw, bdw, wdw, bdw)


def _lru_scan(tile, a_ref, u_ref, h0):
    hq = tile.h
    if not tile.phased:
        hcur = h0
        for q in range(tile.q):
            hcur = a_ref[q * hq:(q + 1) * hq, :] * hcur + u_ref[q * hq:(q + 1) * hq, :]
            u_ref[q * hq:(q + 1) * hq, :] = hcur
        return hcur

    def slab(q, carry):
        acum, z = carry
        r0 = pl.multiple_of(q * hq, hq)
        a = a_ref[pl.ds(r0, hq), :]
        acum = a * acum
        z = a * z + u_ref[pl.ds(r0, hq), :]
        a_ref[pl.ds(r0, hq), :] = acum
        u_ref[pl.ds(r0, hq), :] = z
        return acum, z

    ones = jnp.ones((hq, a_ref.shape[-1]), F32)
    a_end, z_end = lax.fori_loop(0, tile.q, slab, (ones, jnp.zeros_like(ones)), unroll=8)
    starts = []
    hcur = h0
    for p in range(hq):
        starts.append(hcur)
        hcur = a_end[p:p + 1, :] * hcur + z_end[p:p + 1, :]
    start = jnp.concatenate(starts, axis=0)
    c = a_ref.shape[-1]
    hs = a_ref[...].reshape(tile.q, hq, c) * start[None] + u_ref[...].reshape(tile.q, hq, c)
    u_ref[...] = hs.reshape(tile.rows, c)
    return hcur


def _proj_b_body(hp_ref, hs_ref, wx_ref, wg_ref, stp_ref, sts_ref, h0p_ref, h0s_ref, wdw_ref, bdw_ref,
                 wr_ref, br_ref, wi_ref, bi_ref, lam_ref,
                 ybp_ref, nsp_ref, nhp_ref, ybs_ref, nss_ref, nhs_ref,
                 w_scr, ap_ref, up_ref, as_ref, us_ref, *, tile_p, tile_s):
    def one(tile, h_ref, st_ref, h0_ref, yb_ref, ns_ref, nh_ref, a_ref, u_ref):
        r = _dot(h_ref[...], w_scr[...])
        bg = r[:, CT:]
        xb = _conv_value(tile, r[:, :CT], st_ref, ns_ref, wdw_ref, bdw_ref, K_B)

        xb16 = xb.astype(BF16)
        heads = range(CT // HEAD)
        r_lin = jnp.concatenate(
            [_dot(xb16[:, hd * HEAD:(hd + 1) * HEAD], wr_ref[hd].astype(BF16)) for hd in heads], axis=1)
        i_lin = jnp.concatenate(
            [_dot(xb16[:, hd * HEAD:(hd + 1) * HEAD], wi_ref[hd].astype(BF16)) for hd in heads], axis=1)
        rg = _sigmoid(r_lin + br_ref[...])
        ig = _sigmoid(i_lin + bi_ref[...])
        neg_lam = -lam_ref[...]
        softplus = jnp.maximum(neg_lam, 0.0) + jnp.log1p(jnp.exp(-jnp.abs(neg_lam)))
        log_a = (-LRU_C) * rg * softplus
        a_ref[...] = jnp.exp(log_a)
        th = jnp.tanh(log_a)
        u_ref[...] = jnp.sqrt(-2.0 * th / (1.0 - th)) * (ig * xb)

        h_last = _lru_scan(tile, a_ref, u_ref, h0_ref[...].reshape(-1, CT))
        nh_ref[...] = h_last.reshape(nh_ref.shape)
        yb_ref[...] = (_gelu(bg) * u_ref[...]).astype(yb_ref.dtype)

    _cast_pair(w_scr, wx_ref, wg_ref)
    one(tile_p, hp_ref, stp_ref, h0p_ref, ybp_ref, nsp_ref, nhp_ref, ap_ref, up_ref)

    @pl.when(pl.program_id(0) == 0)
    def _():
        one(tile_s, hs_ref, sts_ref, h0s_ref, ybs_ref, nss_ref, nhs_ref, as_ref, us_ref)


def _h0_specs(grp, idx):
    if grp.tile.phased:
        return (pl.BlockSpec((None, 1, 1, CT), lambda *g: (grp.ls, idx(*g)[0], 0, idx(*g)[1])),
                pl.BlockSpec((1, 1, CT), lambda *g: (idx(*g)[0], 0, idx(*g)[1])),
                jax.ShapeDtypeStruct((grp.nseq, 1, W_BR), F32))
    return (pl.BlockSpec((None, grp.tile.h, CT), lambda *g: (grp.ls, 0, idx(*g)[1])),
            pl.BlockSpec((grp.tile.h, CT), lambda *g: (0, idx(*g)[1])),
            jax.ShapeDtypeStruct((grp.nseq, W_BR), F32))


def _proj_b(hp, hs, w_in, l, gp, gs, state_p, state_s, h0p, h0s, wdw, bdw, wr, br, wi, bi, lam):
    mp, d = hp.shape
    ms = hs.shape[0]
    tm = gp.tile.rows
    nj = NJ_BR
    hpt = CT // HEAD
    k1 = K_B - 1
    sj = _first_tile_only(nj)
    idx_p = lambda i, j: (i, j)
    idx_s = lambda i, j: (0, sj(i, j))
    vec = pl.BlockSpec((None, 1, CT), lambda i, j: (l, 0, j))
    nsp_spec, nsp_shape = _state_out(gp, k1, W_BR, idx_p)
    nss_spec, nss_shape = _state_out(gs, k1, W_BR, idx_s)
    h0p_spec, nhp_spec, nhp_shape = _h0_specs(gp, idx_p)
    h0s_spec, nhs_spec, nhs_shape = _h0_specs(gs, idx_s)
    return pl.pallas_call(
        functools.partial(_proj_b_body, tile_p=gp.tile, tile_s=gs.tile),
        grid=(mp // tm, nj),
        in_specs=[_once((tm, d), lambda i, j: (i, 0)),
                  _once((ms, d), lambda i, j: (0, 0)),
                  pl.BlockSpec((None, d, CT), lambda i, j: (l, 0, 2 * nj + j)),
                  pl.BlockSpec((None, d, CT), lambda i, j: (l, 0, 3 * nj + j)),
                  _state_in_spec(gp, k1, idx_p), _state_in_spec(gs, k1, idx_s),
                  h0p_spec, h0s_spec,
                  pl.BlockSpec((None, K_B, CT), lambda i, j: (l, 0, j)),
                  vec,
                  pl.BlockSpec((None, hpt, HEAD, HEAD), lambda i, j: (l, j, 0, 0)),
                  vec,
                  pl.BlockSpec((None, hpt, HEAD, HEAD), lambda i, j: (l, j, 0, 0)),
                  vec,
                  vec],
        out_specs=[pl.BlockSpec((tm, CT), lambda i, j: (i, j)), nsp_spec, nhp_spec,
                   pl.BlockSpec((ms, CT), lambda i, j: (0, sj(i, j))), nss_spec, nhs_spec],
        out_shape=[jax.ShapeDtypeStruct((mp, W_BR), BF16), nsp_shape, nhp_shape,
                   jax.ShapeDtypeStruct((ms, W_BR), BF16), nss_shape, nhs_shape],
        scratch_shapes=[pltpu.VMEM((d, 2 * CT), BF16),
                        pltpu.VMEM((tm, CT), F32), pltpu.VMEM((tm, CT), F32),
                        pltpu.VMEM((ms, CT), F32), pltpu.VMEM((ms, CT), F32)],
        compiler_params=_params(("arbitrary", "arbitrary")),
        name="proj_b",
    )(hp, hs, w_in, w_in, state_p, state_s, h0p, h0s, wdw, bdw, wr, br, wi, bi, lam)


def _merge_body(ucp_ref, ybp_ref, gap_ref, gbp_ref, ucs_ref, ybs_ref, gas_ref, gbs_ref,
                wa_ref, wb_ref, lng_ref, lnb_ref, op_ref, os_ref,
                yap_ref, yas_ref, wa_scr, wb_scr):
    i = pl.program_id(0)
    j = pl.program_id(1)

    def norm_act(uc_ref, ya_ref):
        for r0 in range(0, uc_ref.shape[0], TM_ROWWISE):
            x = uc_ref[r0:r0 + TM_ROWWISE, :]
            mu = jnp.mean(x, axis=-1, keepdims=True)
            xc = x - mu
            var = jnp.mean(xc * xc, axis=-1, keepdims=True)
            y = xc * lax.rsqrt(var + EPS) * lng_ref[...] + lnb_ref[...]
            ya_ref[r0:r0 + TM_ROWWISE, :] = (y * _sigmoid(y)).astype(ya_ref.dtype)

    def one(ya_ref, yb_ref, ga_ref, gb_ref, o_ref):
        out_a = _dot(ya_ref[...], wa_scr[...])
        out_b = _dot(yb_ref[...], wb_scr[...])
        mixed = ga_ref[...].astype(F32) * out_a + gb_ref[...].astype(F32) * out_b
        o_ref[...] = mixed.astype(o_ref.dtype)

    @pl.when(j == 0)
    def _():
        norm_act(ucp_ref, yap_ref)

    @pl.when(jnp.logical_and(i == 0, j == 0))
    def _():
        norm_act(ucs_ref, yas_ref)

    wa_scr[...] = wa_ref[...].astype(BF16)
    wb_scr[...] = wb_ref[...].astype(BF16)
    one(yap_ref, ybp_ref, gap_ref, gbp_ref, op_ref)

    @pl.when(i == 0)
    def _():
        one(yas_ref, ybs_ref, gas_ref, gbs_ref, os_ref)


def _merge(ucp, ybp, gates_p, ucs, ybs, gates_s, wa, wb, ln_g, ln_b, l):
    mp = ucp.shape[0]
    ms = ucs.shape[0]
    tm = TM_MERGE
    nj = D_MODEL // TN
    sj = _first_tile_only(nj)
    return pl.pallas_call(
        _merge_body,
        grid=(mp // tm, nj),
        in_specs=[_once((tm, W_BR), lambda i, j: (i, 0)),
                  _once((tm, W_BR), lambda i, j: (i, 0)),
                  pl.BlockSpec((tm, TN), lambda i, j: (i, j)),
                  pl.BlockSpec((tm, TN), lambda i, j: (i, nj + j)),
                  _once((ms, W_BR), lambda i, j: (0, 0)),
                  _once((ms, W_BR), lambda i, j: (0, 0)),
                  pl.BlockSpec((ms, TN), lambda i, j: (0, sj(i, j))),
                  pl.BlockSpec((ms, TN), lambda i, j: (0, nj + sj(i, j))),
                  pl.BlockSpec((None, W_BR, TN), lambda i, j: (l, 0, j)),
                  pl.BlockSpec((None, W_BR, TN), lambda i, j: (l, 0, j)),
                  pl.BlockSpec((None, 1, W_BR), lambda i, j: (l, 0, 0)),
                  pl.BlockSpec((None, 1, W_BR), lambda i, j: (l, 0, 0))],
        out_specs=[pl.BlockSpec((tm, TN), lambda i, j: (i, j)),
                   pl.BlockSpec((ms, TN), lambda i, j: (0, sj(i, j)))],
        out_shape=[jax.ShapeDtypeStruct((mp, D_MODEL), BF16),
                   jax.ShapeDtypeStruct((ms, D_MODEL), BF16)],
        scratch_shapes=[pltpu.VMEM((tm, W_BR), BF16),
                        pltpu.VMEM((ms, W_BR), BF16),
                        pltpu.VMEM((W_BR, TN), BF16),
                        pltpu.VMEM((W_BR, TN), BF16)],
        compiler_params=_params(("arbitrary", "arbitrary")),
        name="merge",
    )(ucp, ybp, gates_p, gates_p, ucs, ybs, gates_s, gates_s, wa, wb, ln_g, ln_b)


def _ffn_up_body(hp_ref, hs_ref, w1_ref, w2_ref, st1p_ref, st2p_ref, st1s_ref, st2s_ref,
                 wd1p_ref, wd2p_ref, bd1p_ref, bd2p_ref, wd1_ref, wd2_ref, bd1_ref, bd2_ref,
                 gp_ref, ns1p_ref, ns2p_ref, gs_ref, ns1s_ref, ns2s_ref,
                 w_scr, ra_ref, rb_ref, *, tile_p, tile_s):
    s = pl.program_id(0)
    k1 = K_F - 1
    head = k1 * tile_p.h
    rows = min(CONV_ROWS, tile_p.rows)

    @pl.when(s == 0)
    def _():
        rb_ref[...] = jnp.zeros(rb_ref.shape, rb_ref.dtype)

    def step(r_cur, r_prev):
        halves = ((st1p_ref, ns1p_ref, wd1p_ref, bd1p_ref), (st2p_ref, ns2p_ref, wd2p_ref, bd2p_ref))
        for n, (st_ref, ns_ref, _, _) in enumerate(halves):
            body = r_prev.at[pl.ds(head, tile_p.rows), pl.ds(n * CT, CT)]
            r_prev[0:head, n * CT:(n + 1) * CT] = _prev_slabs(tile_p, body, st_ref, k1)
            _store_new_state(tile_p, ns_ref, r_prev[tile_p.rows:, n * CT:(n + 1) * CT], k1)
        tokens = []
        for r0 in range(0, tile_p.rows, rows):
            acc = []
            for n, (_, _, wd_ref, bd_ref) in enumerate(halves):
                a = bd_ref[...] + wd_ref[0:1, :] * r_prev[r0:r0 + rows, n * CT:(n + 1) * CT]
                for k in range(1, K_F):
                    a = a + wd_ref[k:k + 1, :] * r_prev[r0 + k * tile_p.h:r0 + k * tile_p.h + rows,
                                                        n * CT:(n + 1) * CT]
                acc.append(a)
            gp_ref[r0:r0 + rows, :] = (_gelu(acc[0]) * acc[1]).astype(gp_ref.dtype)
            tokens.append(acc[1][0:SUBLANES])

        _cast_pair_after(w_scr, w1_ref, w2_ref, tokens)
        r_cur[head:, :] = _dot(hp_ref[...], w_scr[...])

    @pl.when(s % 2 == 0)
    def _():
        step(ra_ref, rb_ref)

    @pl.when(s % 2 == 1)
    def _():
        step(rb_ref, ra_ref)

    @pl.when(s < NJ_FF)
    def _():
        r = _dot(hs_ref[...], w_scr[...])
        c1 = _conv_value(tile_s, r[:, :CT], st1s_ref, ns1s_ref, wd1_ref, bd1_ref, K_F)
        c2 = _conv_value(tile_s, r[:, CT:], st2s_ref, ns2s_ref, wd2_ref, bd2_ref, K_F)
        gs_ref[...] = (_gelu(c1) * c2).astype(gs_ref.dtype)


def _ffn_up(hp, hs, w_up, l, gp, gs, state_p, state_s, wdw, bdw):
    mp, d = hp.shape
    ms = hs.shape[0]
    tm = gp.tile.rows
    k1 = K_F - 1
    nj = NJ_FF
    ni = mp // tm
    cur, prev = _skewed(ni, nj)
    first = lambda s: (0, jnp.minimum(s, nj - 1))
    hi = lambda idx: (lambda s: (idx(s)[0], nj + idx(s)[1]))
    nsp_spec, nsp_shape = _state_out(gp, k1, D_FF, prev)
    nss_spec, nss_shape = _state_out(gs, k1, D_FF, first)
    wd_spec = lambda idx: pl.BlockSpec((None, K_F, CT), lambda s: (l, 0, idx(s)[1]))
    bd_spec = lambda idx: pl.BlockSpec((None, 1, CT), lambda s: (l, 0, idx(s)[1]))
    r_shape = pltpu.VMEM((k1 * gp.tile.h + tm, 2 * CT), F32)
    return pl.pallas_call(
        functools.partial(_ffn_up_body, tile_p=gp.tile, tile_s=gs.tile),
        grid=(ni * nj + 1,),
        in_specs=[_once((tm, d), lambda s: (cur(s)[0], 0)),
                  _once((ms, d), lambda s: (0, 0)),
                  pl.BlockSpec((None, d, CT), lambda s: (l, 0, cur(s)[1])),
                  pl.BlockSpec((None, d, CT), lambda s: (l, 0, nj + cur(s)[1])),
                  _state_in_spec(gp, k1, prev), _state_in_spec(gp, k1, hi(prev)),
                  _state_in_spec(gs, k1, first), _state_in_spec(gs, k1, hi(first)),
                  wd_spec(prev), wd_spec(hi(prev)), bd_spec(prev), bd_spec(hi(prev)),
                  wd_spec(first), wd_spec(hi(first)), bd_spec(first), bd_spec(hi(first))],
        out_specs=[pl.BlockSpec((tm, CT), lambda s: prev(s)), nsp_spec, nsp_spec,
                   pl.BlockSpec((ms, CT), lambda s: first(s)), nss_spec, nss_spec],
        out_shape=[jax.ShapeDtypeStruct((mp, D_FF), BF16), nsp_shape, nsp_shape,
                   jax.ShapeDtypeStruct((ms, D_FF), BF16), nss_shape, nss_shape],
        scratch_shapes=[pltpu.VMEM((d, 2 * CT), BF16), r_shape, r_shape],
        compiler_params=_params(("arbitrary",)),
        name="ffn_up",
    )(hp, hs, w_up, w_up, state_p, state_p, state_s, state_s,
      wdw, wdw, bdw, bdw, wdw, wdw, bdw, bdw)


def _layer(xp, xs, states_p, states_s, gp, gs, w, l):
    tm = gp.tile.rows
    hp = _rms_cast(xp, w["g_pre_mix"], l)
    hs = _rms_cast(xs, w["g_pre_mix"], l)
    ucp, nap, ucs, nas = _proj_a(hp, hs, w["w_in"], l, gp, gs, states_p[0], states_s[0],
                                 w["w_dw_a"], w["b_dw_a"])
    ybp, nbp, nhp, ybs, nbs, nhs = _proj_b(hp, hs, w["w_in"], l, gp, gs, states_p[1], states_s[1],
                                           states_p[2], states_s[2], w["w_dw_b"], w["b_dw_b"],
                                           w["w_rg_r"], w["b_rg_r"], w["w_rg_i"], w["b_rg_i"],
                                           w["lru_lambda"])
    gtp, gts = _proj(hp, hs, w["w_in"], l, 4 * W_BR, 2 * D_MODEL, tm, _sigmoid, BF16, "proj_gates")
    mxp, mxs = _merge(ucp, ybp, gtp, ucs, ybs, gts, w["w_a_out"], w["w_b_out"],
                      w["ln_a_g"], w["ln_a_b"], l)
    pp, ps = _proj(mxp, mxs, w["w_o"], l, 0, D_MODEL, tm, None, F32, "proj_o")
    x1p, h2p = _resid_rms(xp, pp, w["g_post_mix"], w["g_pre_ffn"], l)
    x1s, h2s = _resid_rms(xs, ps, w["g_post_mix"], w["g_pre_ffn"], l)
    ggp, n1p, n2p, ggs, n1s, n2s = _ffn_up(h2p, h2s, w["w_up"], l, gp, gs, states_p[3], states_s[3],
                                           w["w_dw_f"], w["b_dw_f"])
    fp, fs = _ffn_down(ggp, ggs, w["w_down"], l, tm)
    yp = _resid_rms(x1p, fp, w["g_post_ffn"], None, l)
    ys = _resid_rms(x1s, fs, w["g_post_ffn"], None, l)
    return ((yp, nap, nbp, nhp, jnp.concatenate([n1p, n2p], axis=-1)),
            (ys, nas, nbs, nhs, jnp.concatenate([n1s, n2s], axis=-1)))


def kernel(x_prompt, x_sample, state_conv_a, state_conv_b, state_lru, state_ffn, g_pre_mix, w_in, w_dw_a, b_dw_a, ln_a_g, ln_a_b, w_a_out, w_dw_b, b_dw_b, w_rg_r, b_rg_r, w_rg_i, b_rg_i, lru_lambda, w_b_out, w_o, g_post_mix, g_pre_ffn, w_up, w_dw_f, b_dw_f, w_down, g_post_ffn):
    depth = w_in.shape[0]
    bp, tp, d = x_prompt.shape
    bs, ts, _ = x_sample.shape
    dt = x_prompt.dtype
    w = dict(w_in=w_in, w_dw_a=w_dw_a, w_a_out=w_a_out, w_dw_b=w_dw_b, w_rg_r=w_rg_r,
             w_rg_i=w_rg_i, w_b_out=w_b_out, w_o=w_o, w_up=w_up, w_dw_f=w_dw_f, w_down=w_down)
    vecs = dict(g_pre_mix=g_pre_mix, b_dw_a=b_dw_a, ln_a_g=ln_a_g, ln_a_b=ln_a_b, b_dw_b=b_dw_b,
                b_rg_r=b_rg_r, b_rg_i=b_rg_i, lru_lambda=lru_lambda, g_post_mix=g_post_mix,
                g_pre_ffn=g_pre_ffn, b_dw_f=b_dw_f, g_post_ffn=g_post_ffn)
    w.update({name: v[:, None, :] for name, v in vecs.items()})

    tile_p = Tile(q=tp // SUBLANES, h=SUBLANES, phased=True)
    yp = x_prompt.reshape(bp, SUBLANES, tile_p.q, d).swapaxes(1, 2).reshape(bp * tp, d)
    zeros_p = (jnp.zeros((1, bp, K_A - 1, W_BR), dt), jnp.zeros((1, bp, K_B - 1, W_BR), dt),
               jnp.zeros((1, bp, 1, W_BR), dt), jnp.zeros((1, bp, K_F - 1, 2 * D_FF), dt))
    tile_s = Tile(q=ts, h=bs, phased=False)
    ys = x_sample.swapaxes(0, 1).reshape(ts * bs, d)
    state_s = (state_conv_a.swapaxes(1, 2), state_conv_b.swapaxes(1, 2), state_lru,
               state_ffn.swapaxes(1, 2))

    outs_p = [[], [], [], []]
    outs_s = [[], [], [], []]
    for l in range(depth):
        (yp, *new_p), (ys, *new_s) = _layer(yp, ys, zeros_p, state_s, Group(tile_p, bp, 0),
                                            Group(tile_s, bs, l), w, l)
        new_p[2] = new_p[2][:, 0, :]
        new_s = [new_s[0].swapaxes(0, 1), new_s[1].swapaxes(0, 1), new_s[2], new_s[3].swapaxes(0, 1)]
        for acc, new in ((outs_p, new_p), (outs_s, new_s)):
            for lst, v in zip(acc, new):
                lst.append(v)
    yp = yp.reshape(bp, tile_p.q, SUBLANES, d).swapaxes(1, 2).reshape(bp, tp, d)
    ys = ys.reshape(ts, bs, d).swapaxes(0, 1)
    return (yp, ys, *[jnp.stack(v) for v in outs_p], *[jnp.stack(v) for v in outs_s])
```

```python
import dataclasses
import functools
import math

import jax
import jax.numpy as jnp
from jax import lax
from jax.experimental import pallas as pl
from jax.experimental.pallas import tpu as pltpu

F32 = jnp.float32
BF16 = jnp.bfloat16

D_MODEL = 4096
W_BR = D_MODEL // 2
N_HEADS = 16
HEAD = W_BR // N_HEADS
K_A = 31
K_B = 4
K_F = 3
D_FF = 11008
LRU_C = 8.0
EPS = 1e-6

V7X_VMEM_LIMIT = 62 * 1024 * 1024
SUBLANES = 8
CT = 256
NJ_BR = W_BR // CT
NJ_FF = D_FF // CT
TK_DOWN = 1024
TN_DOWN = 1024
CONV_ROWS = 32
TN = 512
TM_MERGE = 1024
TM_ROWWISE = 256


@dataclasses.dataclass(frozen=True)
class Tile:
    q: int
    h: int
    phased: bool

    @property
    def rows(self):
        return self.q * self.h


@dataclasses.dataclass(frozen=True)
class Group:
    tile: Tile
    nseq: int
    ls: int


def _params(sem):
    return pltpu.CompilerParams(dimension_semantics=sem, vmem_limit_bytes=V7X_VMEM_LIMIT)


def _dot(a, b):
    return jnp.dot(a, b, preferred_element_type=F32)


def _sigmoid(x):
    return 0.5 * (jnp.tanh(0.5 * x) + 1.0)


def _gelu(x):
    c = math.sqrt(2.0 / math.pi)
    return x * (0.5 * (1.0 + jnp.tanh(c * (x + 0.044715 * (x * x * x)))))


def _once(shape, index_map):
    return pl.BlockSpec(shape, index_map, pipeline_mode=pl.Buffered(1))


def _cast_pair(w_scr, wa_ref, wb_ref):
    n = wa_ref.shape[-1]
    w_scr[:, :n] = wa_ref[...].astype(BF16)
    w_scr[:, n:] = wb_ref[...].astype(BF16)


def _zero_bits(v):
    u = lax.bitcast_convert_type(v, jnp.uint32)
    return lax.shift_right_logical(lax.shift_right_logical(u, jnp.uint32(16)), jnp.uint32(16))


def _cast_pair_after(w_scr, wa_ref, wb_ref, tokens):
    k, n = wa_ref.shape
    rb = k // len(tokens)
    for b, tok in enumerate(tokens):
        bits = lax.bitcast_convert_type(wa_ref[b * rb:(b + 1) * rb, :], jnp.uint32)
        bits = bits.reshape(rb // SUBLANES, SUBLANES, n) | _zero_bits(tok)[None]
        w_scr[b * rb:(b + 1) * rb, :n] = lax.bitcast_convert_type(bits.reshape(rb, n), F32).astype(BF16)
    w_scr[:, n:] = wb_ref[...].astype(BF16)


def _first_tile_only(n):
    return lambda i, j: jnp.where(i == 0, j, n - 1)


def _rms_cast_body(x_ref, g_ref, o_ref):
    x = x_ref[...]
    y = x * lax.rsqrt(jnp.mean(x * x, axis=-1, keepdims=True) + EPS)
    o_ref[...] = (y * g_ref[...]).astype(o_ref.dtype)


def _rms_cast(x, g, l):
    m, d = x.shape
    tm = TM_ROWWISE
    return pl.pallas_call(
        _rms_cast_body,
        grid=(m // tm,),
        in_specs=[pl.BlockSpec((tm, d), lambda i: (i, 0)),
                  pl.BlockSpec((None, 1, d), lambda i: (l, 0, 0))],
        out_specs=pl.BlockSpec((tm, d), lambda i: (i, 0)),
        out_shape=jax.ShapeDtypeStruct((m, d), BF16),
        compiler_params=_params(("parallel",)),
        name="rms_cast",
    )(x, g)


def _resid_rms_body(x_ref, p_ref, g_ref, o_ref):
    p = p_ref[...]
    y = p * lax.rsqrt(jnp.mean(p * p, axis=-1, keepdims=True) + EPS)
    o_ref[...] = x_ref[...] + y * g_ref[...]


def _resid_rms_next_body(x_ref, p_ref, g_ref, gn_ref, o_ref, h_ref):
    _resid_rms_body(x_ref, p_ref, g_ref, o_ref)
    x1 = o_ref[...]
    h = x1 * lax.rsqrt(jnp.mean(x1 * x1, axis=-1, keepdims=True) + EPS)
    h_ref[...] = (h * gn_ref[...]).astype(h_ref.dtype)


def _resid_rms(x, p, g, g_next, l):
    m, d = x.shape
    tm = TM_ROWWISE
    row = pl.BlockSpec((tm, d), lambda i: (i, 0))
    vec = pl.BlockSpec((None, 1, d), lambda i: (l, 0, 0))
    if g_next is None:
        return pl.pallas_call(
            _resid_rms_body, grid=(m // tm,), in_specs=[row, row, vec], out_specs=row,
            out_shape=jax.ShapeDtypeStruct((m, d), F32),
            compiler_params=_params(("parallel",)), name="resid_rms",
        )(x, p, g)
    return pl.pallas_call(
        _resid_rms_next_body, grid=(m // tm,), in_specs=[row, row, vec, vec], out_specs=[row, row],
        out_shape=[jax.ShapeDtypeStruct((m, d), F32), jax.ShapeDtypeStruct((m, d), BF16)],
        compiler_params=_params(("parallel",)), name="resid_rms_next",
    )(x, p, g, g_next)


def _proj_body(ap_ref, as_ref, w_ref, op_ref, os_ref, w_scr, *, act):
    def one(a_ref, o_ref):
        r = _dot(a_ref[...], w_scr[...])
        if act is not None:
            r = act(r)
        o_ref[...] = r.astype(o_ref.dtype)

    w_scr[...] = w_ref[...].astype(BF16)
    one(ap_ref, op_ref)

    @pl.when(pl.program_id(0) == 0)
    def _():
        one(as_ref, os_ref)


def _proj(ap, as_, w, l, col0, ncols, tm, act, out_dtype, name):
    mp, d = ap.shape
    ms = as_.shape[0]
    jb = col0 // TN
    nj = ncols // TN
    sj = _first_tile_only(nj)
    return pl.pallas_call(
        functools.partial(_proj_body, act=act),
        grid=(mp // tm, nj),
        in_specs=[_once((tm, d), lambda i, j: (i, 0)),
                  _once((ms, d), lambda i, j: (0, 0)),
                  pl.BlockSpec((None, d, TN), lambda i, j: (l, 0, jb + j))],
        out_specs=[pl.BlockSpec((tm, TN), lambda i, j: (i, j)),
                   pl.BlockSpec((ms, TN), lambda i, j: (0, sj(i, j)))],
        out_shape=[jax.ShapeDtypeStruct((mp, ncols), out_dtype),
                   jax.ShapeDtypeStruct((ms, ncols), out_dtype)],
        scratch_shapes=[pltpu.VMEM((d, TN), BF16)],
        compiler_params=_params(("arbitrary", "arbitrary")),
        name=name,
    )(ap, as_, w)


def _down_body(ap_ref, as_ref, w_ref, op_ref, os_ref, w_scr, *, k_rows):
    k = pl.program_id(2)
    limit = k_rows - k * TK_DOWN
    row = lax.broadcasted_iota(jnp.int32, w_ref.shape, 0)
    col = lax.broadcasted_iota(jnp.int32, (1, TK_DOWN), 1)

    def one(a_ref, o_ref):
        @pl.when(k == 0)
        def _():
            o_ref[...] = jnp.zeros(o_ref.shape, o_ref.dtype)

        a = jnp.where(col < limit, a_ref[...], jnp.zeros((), a_ref.dtype))
        o_ref[...] += _dot(a, w_scr[...])

    w_scr[...] = jnp.where(row < limit, w_ref[...], 0.0).astype(BF16)
    one(ap_ref, op_ref)

    @pl.when(pl.program_id(0) == 0)
    def _():
        one(as_ref, os_ref)


def _ffn_down(gp, gs, w, l, tm):
    mp = gp.shape[0]
    ms = gs.shape[0]
    kd, n = w.shape[1:]
    nk = pl.cdiv(kd, TK_DOWN)
    nj = n // TN_DOWN
    return pl.pallas_call(
        functools.partial(_down_body, k_rows=kd),
        grid=(mp // tm, nj, nk),
        in_specs=[pl.BlockSpec((tm, TK_DOWN), lambda i, j, k: (i, k)),
                  pl.BlockSpec((ms, TK_DOWN), lambda i, j, k: (0, jnp.where(i == 0, k, nk - 1))),
                  pl.BlockSpec((None, TK_DOWN, TN_DOWN), lambda i, j, k: (l, k, j))],
        out_specs=[pl.BlockSpec((tm, TN_DOWN), lambda i, j, k: (i, j)),
                   pl.BlockSpec((ms, TN_DOWN), lambda i, j, k: (0, jnp.where(i == 0, j, nj - 1)))],
        out_shape=[jax.ShapeDtypeStruct((mp, n), F32), jax.ShapeDtypeStruct((ms, n), F32)],
        scratch_shapes=[pltpu.VMEM((TK_DOWN, TN_DOWN), BF16)],
        compiler_params=_params(("arbitrary", "arbitrary", "arbitrary")),
        name="ffn_down",
    )(gp, gs, w)


def _state_in_spec(grp, k1, idx):
    if grp.tile.phased:
        return pl.BlockSpec((None, 1, k1, CT), lambda *g: (grp.ls, idx(*g)[0], 0, idx(*g)[1]))
    return pl.BlockSpec((None, k1, grp.tile.h, CT), lambda *g: (grp.ls, 0, 0, idx(*g)[1]))


def _state_out(grp, k1, width, idx):
    if grp.tile.phased:
        return (pl.BlockSpec((1, k1, CT), lambda *g: (idx(*g)[0], 0, idx(*g)[1])),
                jax.ShapeDtypeStruct((grp.nseq, k1, width), F32))
    return (pl.BlockSpec((k1, grp.tile.h, CT), lambda *g: (0, 0, idx(*g)[1])),
            jax.ShapeDtypeStruct((k1, grp.nseq, width), F32))


def _prev_slabs(tile, x, st_ref, k1):
    if not tile.phased:
        return st_ref[...].reshape(k1 * tile.h, x.shape[-1])
    sub = lax.broadcasted_iota(jnp.int32, (tile.h, x.shape[-1]), 0)
    out = []
    for j in range(k1):
        slab = x[(tile.q - k1 + j) * tile.h:(tile.q - k1 + j + 1) * tile.h]
        out.append(jnp.where(sub == 0, st_ref[0, j:j + 1, :], pltpu.roll(slab, 1, axis=0)))
    return jnp.concatenate(out, axis=0)


def _store_new_state(tile, ns_ref, tail, k1):
    tail3 = tail.reshape(k1, tile.h, tail.shape[-1])
    if tile.phased:
        ns_ref[0] = tail3[:, tile.h - 1, :]
    else:
        ns_ref[...] = tail3


def _conv_value(tile, x, st_ref, ns_ref, w_ref, b_ref, k_taps):
    k1 = k_taps - 1
    xp = jnp.concatenate([_prev_slabs(tile, x, st_ref, k1), x], axis=0)
    _store_new_state(tile, ns_ref, xp[tile.rows:], k1)
    acc = b_ref[...] + w_ref[0:1, :] * xp[0:tile.rows]
    for k in range(1, k_taps):
        acc = acc + w_ref[k:k + 1, :] * xp[k * tile.h:k * tile.h + tile.rows]
    return acc


def _glu_to_scratch(tile, h_ref, w_scr, st_ref, ns_ref, xp_ref):
    k1 = K_A - 1
    head = k1 * tile.h
    r = _dot(h_ref[...], w_scr[...])
    ua = r[:, :CT] * _sigmoid(r[:, CT:])
    xp_ref[0:head, :] = _prev_slabs(tile, ua, st_ref, k1)
    xp_ref[head:, :] = ua
    _store_new_state(tile, ns_ref, xp_ref[tile.rows:, :], k1)


def _proj_a_body(hp_ref, hs_ref, wv_ref, wg_ref, stp_ref, sts_ref, wdw_prev_ref, bdw_prev_ref,
                 wdw_ref, bdw_ref, ucp_ref, nsp_ref, ucs_ref, nss_ref,
                 w_scr, xa_ref, xb_ref, xs_ref, wb_ref, *, tile_p, tile_s):
    s = pl.program_id(0)
    rows = min(CONV_ROWS, tile_p.rows)
    groups = rows // SUBLANES

    @pl.when(s == 0)
    def _():
        xb_ref[...] = jnp.zeros(xb_ref.shape, xb_ref.dtype)

    def step(xp_cur, xp_prev):
        for k in range(K_A):
            wb_ref[k] = jnp.broadcast_to(wdw_prev_ref[k:k + 1, :], (SUBLANES, CT))
        bias = jnp.broadcast_to(bdw_prev_ref[...], (groups, SUBLANES, CT))
        tokens = []
        for r0 in range(0, tile_p.rows, rows):
            acc = bias
            for k in range(K_A):
                x = xp_prev[r0 + k * tile_p.h:r0 + k * tile_p.h + rows, :]
                acc = acc + wb_ref[k][None] * x.reshape(groups, SUBLANES, CT)
            ucp_ref[r0:r0 + rows, :] = acc.reshape(rows, CT)
            tokens.append(acc[0])

        _cast_pair_after(w_scr, wv_ref, wg_ref, tokens)
        _glu_to_scratch(tile_p, hp_ref, w_scr, stp_ref, nsp_ref, xp_cur)

    @pl.when(s % 2 == 0)
    def _():
        step(xa_ref, xb_ref)

    @pl.when(s % 2 == 1)
    def _():
        step(xb_ref, xa_ref)

    @pl.when(s < NJ_BR)
    def _():
        _glu_to_scratch(tile_s, hs_ref, w_scr, sts_ref, nss_ref, xs_ref)
        srows = min(CONV_ROWS, tile_s.rows)
        bias = jnp.broadcast_to(bdw_ref[...], (srows, CT))

        def chunk(c, carry):
            r0 = pl.multiple_of(c * srows, srows)
            acc = bias
            for k in range(K_A):
                acc = acc + wdw_ref[k:k + 1, :] * xs_ref[pl.ds(pl.multiple_of(r0 + k * tile_s.h, SUBLANES), srows), :]
            ucs_ref[pl.ds(r0, srows), :] = acc
            return carry

        lax.fori_loop(0, tile_s.rows // srows, chunk, 0)


def _skewed(ni, nj):
    last = ni * nj - 1

    def cur(s):
        s = jnp.minimum(s, last)
        return s // nj, s % nj

    def prev(s):
        s = jnp.maximum(s - 1, 0)
        return s // nj, s % nj

    return cur, prev


def _proj_a(hp, hs, w_in, l, gp, gs, state_p, state_s, wdw, bdw):
    mp, d = hp.shape
    ms = hs.shape[0]
    tm = gp.tile.rows
    nj = NJ_BR
    ni = mp // tm
    k1 = K_A - 1
    cur, prev = _skewed(ni, nj)
    first = lambda s: (0, jnp.minimum(s, nj - 1))
    nsp_spec, nsp_shape = _state_out(gp, k1, W_BR, cur)
    nss_spec, nss_shape = _state_out(gs, k1, W_BR, first)
    return pl.pallas_call(
        functools.partial(_proj_a_body, tile_p=gp.tile, tile_s=gs.tile),
        grid=(ni * nj + 1,),
        in_specs=[_once((tm, d), lambda s: (cur(s)[0], 0)),
                  _once((ms, d), lambda s: (0, 0)),
                  pl.BlockSpec((None, d, CT), lambda s: (l, 0, cur(s)[1])),
                  pl.BlockSpec((None, d, CT), lambda s: (l, 0, nj + cur(s)[1])),
                  _state_in_spec(gp, k1, cur),
                  _state_in_spec(gs, k1, first),
                  pl.BlockSpec((None, K_A, CT), lambda s: (l, 0, prev(s)[1])),
                  pl.BlockSpec((None, 1, CT), lambda s: (l, 0, prev(s)[1])),
                  pl.BlockSpec((None, K_A, CT), lambda s: (l, 0, first(s)[1])),
                  pl.BlockSpec((None, 1, CT), lambda s: (l, 0, first(s)[1]))],
        out_specs=[pl.BlockSpec((tm, CT), lambda s: prev(s)), nsp_spec,
                   pl.BlockSpec((ms, CT), lambda s: first(s)), nss_spec],
        out_shape=[jax.ShapeDtypeStruct((mp, W_BR), F32), nsp_shape,
                   jax.ShapeDtypeStruct((ms, W_BR), F32), nss_shape],
        scratch_shapes=[pltpu.VMEM((d, 2 * CT), BF16),
                        pltpu.VMEM((k1 * gp.tile.h + tm, CT), F32),
                        pltpu.VMEM((k1 * gp.tile.h + tm, CT), F32),
                        pltpu.VMEM((k1 * gs.tile.h + ms, CT), F32),
                        pltpu.VMEM((K_A, SUBLANES, CT), F32)],
        compiler_params=_params(("arbitrary",)),
        name="proj_a",
    )(hp, hs, w_in, w_in, state_p, state_s, wdw, bdw, wdw, bdw)


def _lru_scan(tile, a_ref, u_ref, h0):
    hq = tile.h
    if not tile.phased:
        hcur = h0
        for q in range(tile.q):
            hcur = a_ref[q * hq:(q + 1) * hq, :] * hcur + u_ref[q * hq:(q + 1) * hq, :]
            u_ref[q * hq:(q + 1) * hq, :] = hcur
        return hcur

    def slab(q, carry):
        acum, z = carry
        r0 = pl.multiple_of(q * hq, hq)
        a = a_ref[pl.ds(r0, hq), :]
        acum = a * acum
        z = a * z + u_ref[pl.ds(r0, hq), :]
        a_ref[pl.ds(r0, hq), :] = acum
        u_ref[pl.ds(r0, hq), :] = z
        return acum, z

    ones = jnp.ones((hq, a_ref.shape[-1]), F32)
    a_end, z_end = lax.fori_loop(0, tile.q, slab, (ones, jnp.zeros_like(ones)), unroll=8)
    starts = []
    hcur = h0
    for p in range(hq):
        starts.append(hcur)
        hcur = a_end[p:p + 1, :] * hcur + z_end[p:p + 1, :]
    start = jnp.concatenate(starts, axis=0)
    c = a_ref.shape[-1]
    hs = a_ref[...].reshape(tile.q, hq, c) * start[None] + u_ref[...].reshape(tile.q, hq, c)
    u_ref[...] = hs.reshape(tile.rows, c)
    return hcur


def _proj_b_body(hp_ref, hs_ref, wx_ref, wg_ref, stp_ref, sts_ref, h0p_ref, h0s_ref, wdw_ref, bdw_ref,
                 wr_ref, br_ref, wi_ref, bi_ref, lam_ref,
                 ybp_ref, nsp_ref, nhp_ref, ybs_ref, nss_ref, nhs_ref,
                 w_scr, ap_ref, up_ref, as_ref, us_ref, *, tile_p, tile_s):
    def one(tile, h_ref, st_ref, h0_ref, yb_ref, ns_ref, nh_ref, a_ref, u_ref):
        r = _dot(h_ref[...], w_scr[...])
        bg = r[:, CT:]
        xb = _conv_value(tile, r[:, :CT], st_ref, ns_ref, wdw_ref, bdw_ref, K_B)

        xb16 = xb.astype(BF16)
        heads = range(CT // HEAD)
        r_lin = jnp.concatenate(
            [_dot(xb16[:, hd * HEAD:(hd + 1) * HEAD], wr_ref[hd].astype(BF16)) for hd in heads], axis=1)
        i_lin = jnp.concatenate(
            [_dot(xb16[:, hd * HEAD:(hd + 1) * HEAD], wi_ref[hd].astype(BF16)) for hd in heads], axis=1)
        rg = _sigmoid(r_lin + br_ref[...])
        ig = _sigmoid(i_lin + bi_ref[...])
        neg_lam = -lam_ref[...]
        softplus = jnp.maximum(neg_lam, 0.0) + jnp.log1p(jnp.exp(-jnp.abs(neg_lam)))
        log_a = (-LRU_C) * rg * softplus
        a_ref[...] = jnp.exp(log_a)
        th = jnp.tanh(log_a)
        u_ref[...] = jnp.sqrt(-2.0 * th / (1.0 - th)) * (ig * xb)

        h_last = _lru_scan(tile, a_ref, u_ref, h0_ref[...].reshape(-1, CT))
        nh_ref[...] = h_last.reshape(nh_ref.shape)
        yb_ref[...] = (_gelu(bg) * u_ref[...]).astype(yb_ref.dtype)

    _cast_pair(w_scr, wx_ref, wg_ref)
    one(tile_p, hp_ref, stp_ref, h0p_ref, ybp_ref, nsp_ref, nhp_ref, ap_ref, up_ref)

    @pl.when(pl.program_id(0) == 0)
    def _():
        one(tile_s, hs_ref, sts_ref, h0s_ref, ybs_ref, nss_ref, nhs_ref, as_ref, us_ref)


def _h0_specs(grp, idx):
    if grp.tile.phased:
        return (pl.BlockSpec((None, 1, 1, CT), lambda *g: (grp.ls, idx(*g)[0], 0, idx(*g)[1])),
                pl.BlockSpec((1, 1, CT), lambda *g: (idx(*g)[0], 0, idx(*g)[1])),
                jax.ShapeDtypeStruct((grp.nseq, 1, W_BR), F32))
    return (pl.BlockSpec((None, grp.tile.h, CT), lambda *g: (grp.ls, 0, idx(*g)[1])),
            pl.BlockSpec((grp.tile.h, CT), lambda *g: (0, idx(*g)[1])),
            jax.ShapeDtypeStruct((grp.nseq, W_BR), F32))


def _proj_b(hp, hs, w_in, l, gp, gs, state_p, state_s, h0p, h0s, wdw, bdw, wr, br, wi, bi, lam):
    mp, d = hp.shape
    ms = hs.shape[0]
    tm = gp.tile.rows
    nj = NJ_BR
    hpt = CT // HEAD
    k1 = K_B - 1
    sj = _first_tile_only(nj)
    idx_p = lambda i, j: (i, j)
    idx_s = lambda i, j: (0, sj(i, j))
    vec = pl.BlockSpec((None, 1, CT), lambda i, j: (l, 0, j))
    nsp_spec, nsp_shape = _state_out(gp, k1, W_BR, idx_p)
    nss_spec, nss_shape = _state_out(gs, k1, W_BR, idx_s)
    h0p_spec, nhp_spec, nhp_shape = _h0_specs(gp, idx_p)
    h0s_spec, nhs_spec, nhs_shape = _h0_specs(gs, idx_s)
    return pl.pallas_call(
        functools.partial(_proj_b_body, tile_p=gp.tile, tile_s=gs.tile),
        grid=(mp // tm, nj),
        in_specs=[_once((tm, d), lambda i, j: (i, 0)),
                  _once((ms, d), lambda i, j: (0, 0)),
                  pl.BlockSpec((None, d, CT), lambda i, j: (l, 0, 2 * nj + j)),
                  pl.BlockSpec((None, d, CT), lambda i, j: (l, 0, 3 * nj + j)),
                  _state_in_spec(gp, k1, idx_p), _state_in_spec(gs, k1, idx_s),
                  h0p_spec, h0s_spec,
                  pl.BlockSpec((None, K_B, CT), lambda i, j: (l, 0, j)),
                  vec,
                  pl.BlockSpec((None, hpt, HEAD, HEAD), lambda i, j: (l, j, 0, 0)),
                  vec,
                  pl.BlockSpec((None, hpt, HEAD, HEAD), lambda i, j: (l, j, 0, 0)),
                  vec,
                  vec],
        out_specs=[pl.BlockSpec((tm, CT), lambda i, j: (i, j)), nsp_spec, nhp_spec,
                   pl.BlockSpec((ms, CT), lambda i, j: (0, sj(i, j))), nss_spec, nhs_spec],
        out_shape=[jax.ShapeDtypeStruct((mp, W_BR), BF16), nsp_shape, nhp_shape,
                   jax.ShapeDtypeStruct((ms, W_BR), BF16), nss_shape, nhs_shape],
        scratch_shapes=[pltpu.VMEM((d, 2 * CT), BF16),
                        pltpu.VMEM((tm, CT), F32), pltpu.VMEM((tm, CT), F32),
                        pltpu.VMEM((ms, CT), F32), pltpu.VMEM((ms, CT), F32)],
        compiler_params=_params(("arbitrary", "arbitrary")),
        name="proj_b",
    )(hp, hs, w_in, w_in, state_p, state_s, h0p, h0s, wdw, bdw, wr, br, wi, bi, lam)


def _merge_body(ucp_ref, ybp_ref, gap_ref, gbp_ref, ucs_ref, ybs_ref, gas_ref, gbs_ref,
                wa_ref, wb_ref, lng_ref, lnb_ref, op_ref, os_ref,
                yap_ref, yas_ref, wa_scr, wb_scr):
    i = pl.program_id(0)
    j = pl.program_id(1)

    def norm_act(uc_ref, ya_ref):
        for r0 in range(0, uc_ref.shape[0], TM_ROWWISE):
            x = uc_ref[r0:r0 + TM_ROWWISE, :]
            mu = jnp.mean(x, axis=-1, keepdims=True)
            xc = x - mu
            var = jnp.mean(xc * xc, axis=-1, keepdims=True)
            y = xc * lax.rsqrt(var + EPS) * lng_ref[...] + lnb_ref[...]
            ya_ref[r0:r0 + TM_ROWWISE, :] = (y * _sigmoid(y)).astype(ya_ref.dtype)

    def one(ya_ref, yb_ref, ga_ref, gb_ref, o_ref):
        out_a = _dot(ya_ref[...], wa_scr[...])
        out_b = _dot(yb_ref[...], wb_scr[...])
        mixed = ga_ref[...].astype(F32) * out_a + gb_ref[...].astype(F32) * out_b
        o_ref[...] = mixed.astype(o_ref.dtype)

    @pl.when(j == 0)
    def _():
        norm_act(ucp_ref, yap_ref)

    @pl.when(jnp.logical_and(i == 0, j == 0))
    def _():
        norm_act(ucs_ref, yas_ref)

    wa_scr[...] = wa_ref[...].astype(BF16)
    wb_scr[...] = wb_ref[...].astype(BF16)
    one(yap_ref, ybp_ref, gap_ref, gbp_ref, op_ref)

    @pl.when(i == 0)
    def _():
        one(yas_ref, ybs_ref, gas_ref, gbs_ref, os_ref)


def _merge(ucp, ybp, gates_p, ucs, ybs, gates_s, wa, wb, ln_g, ln_b, l):
    mp = ucp.shape[0]
    ms = ucs.shape[0]
    tm = TM_MERGE
    nj = D_MODEL // TN
    sj = _first_tile_only(nj)
    return pl.pallas_call(
        _merge_body,
        grid=(mp // tm, nj),
        in_specs=[_once((tm, W_BR), lambda i, j: (i, 0)),
                  _once((tm, W_BR), lambda i, j: (i, 0)),
                  pl.BlockSpec((tm, TN), lambda i, j: (i, j)),
                  pl.BlockSpec((tm, TN), lambda i, j: (i, nj + j)),
                  _once((ms, W_BR), lambda i, j: (0, 0)),
                  _once((ms, W_BR), lambda i, j: (0, 0)),
                  pl.BlockSpec((ms, TN), lambda i, j: (0, sj(i, j))),
                  pl.BlockSpec((ms, TN), lambda i, j: (0, nj + sj(i, j))),
                  pl.BlockSpec((None, W_BR, TN), lambda i, j: (l, 0, j)),
                  pl.BlockSpec((None, W_BR, TN), lambda i, j: (l, 0, j)),
                  pl.BlockSpec((None, 1, W_BR), lambda i, j: (l, 0, 0)),
                  pl.BlockSpec((None, 1, W_BR), lambda i, j: (l, 0, 0))],
        out_specs=[pl.BlockSpec((tm, TN), lambda i, j: (i, j)),
                   pl.BlockSpec((ms, TN), lambda i, j: (0, sj(i, j)))],
        out_shape=[jax.ShapeDtypeStruct((mp, D_MODEL), BF16),
                   jax.ShapeDtypeStruct((ms, D_MODEL), BF16)],
        scratch_shapes=[pltpu.VMEM((tm, W_BR), BF16),
                        pltpu.VMEM((ms, W_BR), BF16),
                        pltpu.VMEM((W_BR, TN), BF16),
                        pltpu.VMEM((W_BR, TN), BF16)],
        compiler_params=_params(("arbitrary", "arbitrary")),
        name="merge",
    )(ucp, ybp, gates_p, gates_p, ucs, ybs, gates_s, gates_s, wa, wb, ln_g, ln_b)


def _ffn_up_body(hp_ref, hs_ref, w1_ref, w2_ref, st1p_ref, st2p_ref, st1s_ref, st2s_ref,
                 wd1p_ref, wd2p_ref, bd1p_ref, bd2p_ref, wd1_ref, wd2_ref, bd1_ref, bd2_ref,
                 gp_ref, ns1p_ref, ns2p_ref, gs_ref, ns1s_ref, ns2s_ref,
                 w_scr, ra_ref, rb_ref, *, tile_p, tile_s):
    s = pl.program_id(0)
    k1 = K_F - 1
    head = k1 * tile_p.h
    rows = min(CONV_ROWS, tile_p.rows)

    @pl.when(s == 0)
    def _():
        rb_ref[...] = jnp.zeros(rb_ref.shape, rb_ref.dtype)

    def step(r_cur, r_prev):
        halves = ((st1p_ref, ns1p_ref, wd1p_ref, bd1p_ref), (st2p_ref, ns2p_ref, wd2p_ref, bd2p_ref))
        for n, (st_ref, ns_ref, _, _) in enumerate(halves):
            body = r_prev.at[pl.ds(head, tile_p.rows), pl.ds(n * CT, CT)]
            r_prev[0:head, n * CT:(n + 1) * CT] = _prev_slabs(tile_p, body, st_ref, k1)
            _store_new_state(tile_p, ns_ref, r_prev[tile_p.rows:, n * CT:(n + 1) * CT], k1)
        tokens = []
        for r0 in range(0, tile_p.rows, rows):
            acc = []
            for n, (_, _, wd_ref, bd_ref) in enumerate(halves):
                a = bd_ref[...] + wd_ref[0:1, :] * r_prev[r0:r0 + rows, n * CT:(n + 1) * CT]
                for k in range(1, K_F):
                    a = a + wd_ref[k:k + 1, :] * r_prev[r0 + k * tile_p.h:r0 + k * tile_p.h + rows,
                                                        n * CT:(n + 1) * CT]
                acc.append(a)
            gp_ref[r0:r0 + rows, :] = (_gelu(acc[0]) * acc[1]).astype(gp_ref.dtype)
            tokens.append(acc[1][0:SUBLANES])

        _cast_pair_after(w_scr, w1_ref, w2_ref, tokens)
        r_cur[head:, :] = _dot(hp_ref[...], w_scr[...])

    @pl.when(s % 2 == 0)
    def _():
        step(ra_ref, rb_ref)

    @pl.when(s % 2 == 1)
    def _():
        step(rb_ref, ra_ref)

    @pl.when(s < NJ_FF)
    def _():
        r = _dot(hs_ref[...], w_scr[...])
        c1 = _conv_value(tile_s, r[:, :CT], st1s_ref, ns1s_ref, wd1_ref, bd1_ref, K_F)
        c2 = _conv_value(tile_s, r[:, CT:], st2s_ref, ns2s_ref, wd2_ref, bd2_ref, K_F)
        gs_ref[...] = (_gelu(c1) * c2).astype(gs_ref.dtype)


def _ffn_up(hp, hs, w_up, l, gp, gs, state_p, state_s, wdw, bdw):
    mp, d = hp.shape
    ms = hs.shape[0]
    tm = gp.tile.rows
    k1 = K_F - 1
    nj = NJ_FF
    ni = mp // tm
    cur, prev = _skewed(ni, nj)
    first = lambda s: (0, jnp.minimum(s, nj - 1))
    hi = lambda idx: (lambda s: (idx(s)[0], nj + idx(s)[1]))
    nsp_spec, nsp_shape = _state_out(gp, k1, D_FF, prev)
    nss_spec, nss_shape = _state_out(gs, k1, D_FF, first)
    wd_spec = lambda idx: pl.BlockSpec((None, K_F, CT), lambda s: (l, 0, idx(s)[1]))
    bd_spec = lambda idx: pl.BlockSpec((None, 1, CT), lambda s: (l, 0, idx(s)[1]))
    r_shape = pltpu.VMEM((k1 * gp.tile.h + tm, 2 * CT), F32)
    return pl.pallas_call(
        functools.partial(_ffn_up_body, tile_p=gp.tile, tile_s=gs.tile),
        grid=(ni * nj + 1,),
        in_specs=[_once((tm, d), lambda s: (cur(s)[0], 0)),
                  _once((ms, d), lambda s: (0, 0)),
                  pl.BlockSpec((None, d, CT), lambda s: (l, 0, cur(s)[1])),
                  pl.BlockSpec((None, d, CT), lambda s: (l, 0, nj + cur(s)[1])),
                  _state_in_spec(gp, k1, prev), _state_in_spec(gp, k1, hi(prev)),
                  _state_in_spec(gs, k1, first), _state_in_spec(gs, k1, hi(first)),
                  wd_spec(prev), wd_spec(hi(prev)), bd_spec(prev), bd_spec(hi(prev)),
                  wd_spec(first), wd_spec(hi(first)), bd_spec(first), bd_spec(hi(first))],
        out_specs=[pl.BlockSpec((tm, CT), lambda s: prev(s)), nsp_spec, nsp_spec,
                   pl.BlockSpec((ms, CT), lambda s: first(s)), nss_spec, nss_spec],
        out_shape=[jax.ShapeDtypeStruct((mp, D_FF), BF16), nsp_shape, nsp_shape,
                   jax.ShapeDtypeStruct((ms, D_FF), BF16), nss_shape, nss_shape],
        scratch_shapes=[pltpu.VMEM((d, 2 * CT), BF16), r_shape, r_shape],
        compiler_params=_params(("arbitrary",)),
        name="ffn_up",
    )(hp, hs, w_up, w_up, state_p, state_p, state_s, state_s,
      wdw, wdw, bdw, bdw, wdw, wdw, bdw, bdw)


def _layer(xp, xs, states_p, states_s, gp, gs, w, l):
    tm = gp.tile.rows
    hp = _rms_cast(xp, w["g_pre_mix"], l)
    hs = _rms_cast(xs, w["g_pre_mix"], l)
    ucp, nap, ucs, nas = _proj_a(hp, hs, w["w_in"], l, gp, gs, states_p[0], states_s[0],
                                 w["w_dw_a"], w["b_dw_a"])
    ybp, nbp, nhp, ybs, nbs, nhs = _proj_b(hp, hs, w["w_in"], l, gp, gs, states_p[1], states_s[1],
                                           states_p[2], states_s[2], w["w_dw_b"], w["b_dw_b"],
                                           w["w_rg_r"], w["b_rg_r"], w["w_rg_i"], w["b_rg_i"],
                                           w["lru_lambda"])
    gtp, gts = _proj(hp, hs, w["w_in"], l, 4 * W_BR, 2 * D_MODEL, tm, _sigmoid, BF16, "proj_gates")
    mxp, mxs = _merge(ucp, ybp, gtp, ucs, ybs, gts, w["w_a_out"], w["w_b_out"],
                      w["ln_a_g"], w["ln_a_b"], l)
    pp, ps = _proj(mxp, mxs, w["w_o"], l, 0, D_MODEL, tm, None, F32, "proj_o")
    x1p, h2p = _resid_rms(xp, pp, w["g_post_mix"], w["g_pre_ffn"], l)
    x1s, h2s = _resid_rms(xs, ps, w["g_post_mix"], w["g_pre_ffn"], l)
    ggp, n1p, n2p, ggs, n1s, n2s = _ffn_up(h2p, h2s, w["w_up"], l, gp, gs, states_p[3], states_s[3],
                                           w["w_dw_f"], w["b_dw_f"])
    fp, fs = _ffn_down(ggp, ggs, w["w_down"], l, tm)
    yp = _resid_rms(x1p, fp, w["g_post_ffn"], None, l)
    ys = _resid_rms(x1s, fs, w["g_post_ffn"], None, l)
    return ((yp, nap, nbp, nhp, jnp.concatenate([n1p, n2p], axis=-1)),
            (ys, nas, nbs, nhs, jnp.concatenate([n1s, n2s], axis=-1)))


def kernel(x_prompt, x_sample, state_conv_a, state_conv_b, state_lru, state_ffn, g_pre_mix, w_in, w_dw_a, b_dw_a, ln_a_g, ln_a_b, w_a_out, w_dw_b, b_dw_b, w_rg_r, b_rg_r, w_rg_i, b_rg_i, lru_lambda, w_b_out, w_o, g_post_mix, g_pre_ffn, w_up, w_dw_f, b_dw_f, w_down, g_post_ffn):
    depth = w_in.shape[0]
    bp, tp, d = x_prompt.shape
    bs, ts, _ = x_sample.shape
    dt = x_prompt.dtype
    w = dict(w_in=w_in, w_dw_a=w_dw_a, w_a_out=w_a_out, w_dw_b=w_dw_b, w_rg_r=w_rg_r,
             w_rg_i=w_rg_i, w_b_out=w_b_out, w_o=w_o, w_up=w_up, w_dw_f=w_dw_f, w_down=w_down)
    vecs = dict(g_pre_mix=g_pre_mix, b_dw_a=b_dw_a, ln_a_g=ln_a_g, ln_a_b=ln_a_b, b_dw_b=b_dw_b,
                b_rg_r=b_rg_r, b_rg_i=b_rg_i, lru_lambda=lru_lambda, g_post_mix=g_post_mix,
                g_pre_ffn=g_pre_ffn, b_dw_f=b_dw_f, g_post_ffn=g_post_ffn)
    w.update({name: v[:, None, :] for name, v in vecs.items()})

    tile_p = Tile(q=tp // SUBLANES, h=SUBLANES, phased=True)
    yp = x_prompt.reshape(bp, SUBLANES, tile_p.q, d).swapaxes(1, 2).reshape(bp * tp, d)
    zeros_p = (jnp.zeros((1, bp, K_A - 1, W_BR), dt), jnp.zeros((1, bp, K_B - 1, W_BR), dt),
               jnp.zeros((1, bp, 1, W_BR), dt), jnp.zeros((1, bp, K_F - 1, 2 * D_FF), dt))
    tile_s = Tile(q=ts, h=bs, phased=False)
    ys = x_sample.swapaxes(0, 1).reshape(ts * bs, d)
    state_s = (state_conv_a.swapaxes(1, 2), state_conv_b.swapaxes(1, 2), state_lru,
               state_ffn.swapaxes(1, 2))

    outs_p = [[], [], [], []]
    outs_s = [[], [], [], []]
    for l in range(depth):
        (yp, *new_p), (ys, *new_s) = _layer(yp, ys, zeros_p, state_s, Group(tile_p, bp, 0),
                                            Group(tile_s, bs, l), w, l)
        new_p[2] = new_p[2][:, 0, :]
        new_s = [new_s[0].swapaxes(0, 1), new_s[1].swapaxes(0, 1), new_s[2], new_s[3].swapaxes(0, 1)]
        for acc, new in ((outs_p, new_p), (outs_s, new_s)):
            for lst, v in zip(acc, new):
                lst.append(v)
    yp = yp.reshape(bp, tile_p.q, SUBLANES, d).swapaxes(1, 2).reshape(bp, tp, d)
    ys = ys.reshape(ts, bs, d).swapaxes(0, 1)
    return (yp, ys, *[jnp.stack(v) for v in outs_p], *[jnp.stack(v) for v in outs_s])
```
